```python
import math
import jax, jax.numpy as jnp
from jax import lax
import numpy as np

D_MODEL = 1024
BATCH = 4
SEQ = 4096
DEPTH = 4

CHUNK = 64
N_A_LAYERS = DEPTH // 2
N_B_LAYERS = DEPTH - N_A_LAYERS
SSM_GROUP = 16
SSM_GROUPS = D_MODEL // SSM_GROUP
SSM_STATE = 64
N_HEADS = 16
HEAD_DIM = D_MODEL // N_HEADS
Q_BLOCK = 128
PEER_HEADS = 8
PEER_NKEYS = 128
PEER_EXPERTS = PEER_NKEYS * PEER_NKEYS
PEER_TOPK = 16
PEER_QDIM = 256
PEER_TOK_BLOCK = 128
PLE_DIM = 256
ALPHA = (2 * DEPTH) ** 0.25
BETA = (8 * DEPTH) ** -0.25
LN_EPS = 1e-5

kernel_name = "yoco_s5_stickbreak_peer_trunk"


def layer_norm(h, g, b):
    hf = h.astype(jnp.float32)
    mu = jnp.mean(hf, axis=-1, keepdims=True)
    var = jnp.mean(jnp.square(hf - mu), axis=-1, keepdims=True)
    y = (hf - mu) * lax.rsqrt(var + LN_EPS)
    return (y * g.astype(jnp.float32) + b.astype(jnp.float32)).astype(h.dtype)


def s5_mixer(h, w_in, lam_re, lam_im, log_dt, b_re, b_im, c_re, c_im, d_skip, w_glu):
    bsz, seq, _ = h.shape
    f32 = jnp.float32
    u = (h @ w_in).astype(f32)
    ug = u.reshape(bsz, seq, SSM_GROUPS, SSM_GROUP)
    lam = lax.complex(lam_re.astype(f32), lam_im.astype(f32))
    dt = jnp.exp(log_dt.astype(f32))[:, None]
    a_bar = jnp.exp(lam * dt)
    b_cplx = lax.complex(b_re.astype(f32), b_im.astype(f32))
    b_bar = ((a_bar - 1.0) / lam)[..., None] * b_cplx
    bu = lax.complex(jnp.einsum('bsgc,gpc->bsgp', ug, jnp.real(b_bar)),
                     jnp.einsum('bsgc,gpc->bsgp', ug, jnp.imag(b_bar)))
    a_seq = jnp.broadcast_to(a_bar[None, None], (1, seq, SSM_GROUPS, SSM_STATE))

    def combine(left, right):
        a_l, b_l = left
        a_r, b_r = right
        return a_l * a_r, a_r * b_l + b_r

    _, states = lax.associative_scan(combine, (a_seq, bu), axis=1)
    y = (jnp.einsum('bsgp,gcp->bsgc', jnp.real(states), c_re.astype(f32))
         - jnp.einsum('bsgp,gcp->bsgc', jnp.imag(states), c_im.astype(f32)))
    y = y.reshape(bsz, seq, D_MODEL) + d_skip.astype(f32) * u
    y = jax.nn.gelu(y).astype(h.dtype)
    val, gate = jnp.split(y @ w_glu, 2, axis=-1)
    return val * jax.nn.sigmoid(gate)


def shared_kv(h, g, b, w_kv):
    bsz, seq, _ = h.shape
    k, v = jnp.split(layer_norm(h, g, b) @ w_kv, 2, axis=-1)
    return (k.reshape(bsz, seq, N_HEADS, HEAD_DIM), v.reshape(bsz, seq, N_HEADS, HEAD_DIM))


def stick_breaking_mixer(h, w_q, w_o, k, v):
    bsz, seq, _ = h.shape
    q = (h @ w_q).reshape(bsz, seq, N_HEADS, HEAD_DIM)
    scale = HEAD_DIM ** -0.5
    outs = []
    for blk in range(seq // Q_BLOCK):
        t0 = blk * Q_BLOCK
        t1 = t0 + Q_BLOCK
        qb, kb, vb = q[:, t0:t1], k[:, :t1], v[:, :t1]
        z = jnp.einsum('bqhd,bkhd->bhqk', qb, kb).astype(jnp.float32) * scale
        mask = jnp.arange(t1)[None, :] < jnp.arange(t0, t1)[:, None]
        log_keep = jnp.where(mask, jax.nn.log_sigmoid(-z), 0.0)
        later = lax.cumsum(log_keep, axis=3, reverse=True) - log_keep
        w = jnp.where(mask, jnp.exp(jax.nn.log_sigmoid(z) + later), 0.0)
        outs.append(jnp.einsum('bhqk,bkhd->bqhd', w.astype(vb.dtype), vb))
    o = jnp.concatenate(outs, axis=1).reshape(bsz, seq, D_MODEL)
    return o @ w_o


def peer_ffn(h, w_q, k1, k2, u_tab, v_tab):
    bsz, seq, _ = h.shape
    n_tok = bsz * seq
    xt = h.reshape(n_tok, D_MODEL)
    q = (xt @ w_q).reshape(n_tok, PEER_HEADS, PEER_QDIM).astype(jnp.float32)
    q1, q2 = jnp.split(q, 2, axis=-1)
    s1 = jnp.einsum('thd,nd->thn', q1, k1.astype(jnp.float32))
    s2 = jnp.einsum('thd,nd->thn', q2, k2.astype(jnp.float32))
    v1, i1 = lax.top_k(s1, PEER_TOPK)
    v2, i2 = lax.top_k(s2, PEER_TOPK)
    cand = (v1[..., :, None] + v2[..., None, :]).reshape(n_tok, PEER_HEADS, PEER_TOPK * PEER_TOPK)
    sc, ci = lax.top_k(cand, PEER_TOPK)
    e1 = jnp.take_along_axis(i1, ci // PEER_TOPK, axis=-1)
    e2 = jnp.take_along_axis(i2, ci % PEER_TOPK, axis=-1)
    idx = e1 * PEER_NKEYS + e2
    gates = jax.nn.softmax(sc, axis=-1)
    n_blk = n_tok // PEER_TOK_BLOCK

    def block(args):
        xb, ib, gb = args
        act = jnp.einsum('td,thkd->thk', xb, u_tab[ib]).astype(jnp.float32)
        wgt = (gb * jax.nn.gelu(act)).astype(xb.dtype)
        return jnp.einsum('thk,thkd->td', wgt, v_tab[ib])

    out = lax.map(block, (xt.reshape(n_blk, PEER_TOK_BLOCK, D_MODEL),
                          idx.reshape(n_blk, PEER_TOK_BLOCK, PEER_HEADS, PEER_TOPK),
                          gates.reshape(n_blk, PEER_TOK_BLOCK, PEER_HEADS, PEER_TOPK)))
    return out.reshape(bsz, seq, D_MODEL).astype(h.dtype)


def setup_inputs(seed: int = 0) -> dict:
    key = jax.random.key(seed)
    ks = jax.random.split(key, 32)
    f32 = jnp.float32

    def nrm(k, shape, std):
        return std * jax.random.normal(k, shape, f32)

    D, G, P = D_MODEL, SSM_GROUPS, SSM_STATE
    x = nrm(ks[0], (BATCH, SEQ, D), 1.0)
    p = nrm(ks[1], (DEPTH, BATCH, SEQ, PLE_DIM), 1.0)
    ln_mix_g = 1.0 + nrm(ks[2], (DEPTH, D), 0.02)
    ln_mix_b = nrm(ks[3], (DEPTH, D), 0.02)
    ln_ffn_g = 1.0 + nrm(ks[4], (DEPTH, D), 0.02)
    ln_ffn_b = nrm(ks[5], (DEPTH, D), 0.02)
    ssm_w_in = nrm(ks[6], (N_A_LAYERS, D, D), D ** -0.5)
    ssm_lam_re = -0.5 * jnp.exp(nrm(ks[7], (N_A_LAYERS, G, P), 0.01))
    ssm_lam_im = math.pi * jnp.arange(P, dtype=f32)[None, None, :] + nrm(ks[8], (N_A_LAYERS, G, P), 0.01)
    ssm_log_dt = jax.random.uniform(ks[9], (N_A_LAYERS, G), f32, math.log(1e-3), math.log(1e-1))
    ssm_b_re = nrm(ks[10], (N_A_LAYERS, G, P, SSM_GROUP), (2 * SSM_GROUP) ** -0.5)
    ssm_b_im = nrm(ks[11], (N_A_LAYERS, G, P, SSM_GROUP), (2 * SSM_GROUP) ** -0.5)
    ssm_c_re = nrm(ks[12], (N_A_LAYERS, G, SSM_GROUP, P), P ** -0.5)
    ssm_c_im = nrm(ks[13], (N_A_LAYERS, G, SSM_GROUP, P), P ** -0.5)
    ssm_d = nrm(ks[14], (N_A_LAYERS, D), 1.0)
    ssm_w_glu = jnp.concatenate([nrm(ks[15], (N_A_LAYERS, D, D), BETA * D ** -0.5),
                                 nrm(ks[16], (N_A_LAYERS, D, D), D ** -0.5)], axis=-1)
    kv_ln_g = 1.0 + nrm(ks[17], (D,), 0.02)
    kv_ln_b = nrm(ks[18], (D,), 0.02)
    w_kv = jnp.concatenate([nrm(ks[19], (D, D), D ** -0.5),
                            nrm(ks[20], (D, D), BETA * D ** -0.5)], axis=-1)
    sb_w_q = nrm(ks[21], (N_B_LAYERS, D, D), D ** -0.5)
    sb_w_o = nrm(ks[22], (N_B_LAYERS, D, D), BETA * D ** -0.5)
    peer_w_q = nrm(ks[23], (DEPTH, D, PEER_HEADS * PEER_QDIM), D ** -0.5)
    peer_k1 = nrm(ks[24], (DEPTH, PEER_NKEYS, PEER_QDIM // 2), (PEER_QDIM // 2) ** -0.5)
    peer_k2 = nrm(ks[25], (DEPTH, PEER_NKEYS, PEER_QDIM // 2), (PEER_QDIM // 2) ** -0.5)
    peer_u = nrm(ks[26], (DEPTH, PEER_EXPERTS, D), D ** -0.5)
    peer_v = nrm(ks[27], (DEPTH, PEER_EXPERTS, D), BETA)
    ple_w = nrm(ks[28], (DEPTH, PLE_DIM, D), PLE_DIM ** -0.5)
    ple_gate = nrm(ks[29], (DEPTH, D, D), D ** -0.5)
    return {"x": x, "p": p, "ln_mix_g": ln_mix_g, "ln_mix_b": ln_mix_b,
            "ln_ffn_g": ln_ffn_g, "ln_ffn_b": ln_ffn_b, "ssm_w_in": ssm_w_in,
            "ssm_lam_re": ssm_lam_re, "ssm_lam_im": ssm_lam_im, "ssm_log_dt": ssm_log_dt,
            "ssm_b_re": ssm_b_re, "ssm_b_im": ssm_b_im, "ssm_c_re": ssm_c_re,
            "ssm_c_im": ssm_c_im, "ssm_d": ssm_d, "ssm_w_glu": ssm_w_glu,
            "kv_ln_g": kv_ln_g, "kv_ln_b": kv_ln_b, "w_kv": w_kv,
            "sb_w_q": sb_w_q, "sb_w_o": sb_w_o, "peer_w_q": peer_w_q,
            "peer_k1": peer_k1, "peer_k2": peer_k2, "peer_u": peer_u, "peer_v": peer_v,
            "ple_w": ple_w, "ple_gate": ple_gate}


def reference(x, p, ln_mix_g, ln_mix_b, ln_ffn_g, ln_ffn_b, ssm_w_in, ssm_lam_re, ssm_lam_im,
              ssm_log_dt, ssm_b_re, ssm_b_im, ssm_c_re, ssm_c_im, ssm_d, ssm_w_glu,
              kv_ln_g, kv_ln_b, w_kv, sb_w_q, sb_w_o, peer_w_q, peer_k1, peer_k2,
              peer_u, peer_v, ple_w, ple_gate):
    h = x
    k_sh = None
    v_sh = None
    for i in range(DEPTH):
        if i < N_A_LAYERS:
            mix = s5_mixer(h, ssm_w_in[i], ssm_lam_re[i], ssm_lam_im[i], ssm_log_dt[i],
                           ssm_b_re[i], ssm_b_im[i], ssm_c_re[i], ssm_c_im[i],
                           ssm_d[i], ssm_w_glu[i])
        else:
            if i == N_A_LAYERS:
                k_sh, v_sh = shared_kv(h, kv_ln_g, kv_ln_b, w_kv)
            j = i - N_A_LAYERS
            mix = stick_breaking_mixer(h, sb_w_q[j], sb_w_o[j], k_sh, v_sh)
        h = layer_norm(ALPHA * h + mix, ln_mix_g[i], ln_mix_b[i])
        ffn = peer_ffn(h, peer_w_q[i], peer_k1[i], peer_k2[i], peer_u[i], peer_v[i])
        h = layer_norm(ALPHA * h + ffn, ln_ffn_g[i], ln_ffn_b[i])
        h = h + jax.nn.sigmoid(h @ ple_gate[i]) * (p[i] @ ple_w[i])
    return h
```

```python
import functools
import math

import jax
import jax.numpy as jnp
from jax import lax
from jax.experimental import pallas as pl
from jax.experimental.pallas import tpu as pltpu

F32 = jnp.float32
BF16 = jnp.bfloat16

N_HEADS = 16
HEAD_DIM = 64
SSM_GROUP = 16
SSM_STATE = 64
SSM_CHUNK = 16
PEER_HEADS = 8
PEER_NKEYS = 128
PEER_TOPK = 16
LN_EPS = 1e-5

LANES = 128
SUBLANES = 8
VMEM_LIMIT = 56 * 1024 * 1024

_NT = (((1,), (1,)), ((), ()))


def _params(*sem):
    return pltpu.CompilerParams(dimension_semantics=sem, vmem_limit_bytes=VMEM_LIMIT)


def _ln(x, g, b):
    mu = jnp.mean(x, axis=-1, keepdims=True)
    xc = x - mu
    var = jnp.mean(xc * xc, axis=-1, keepdims=True)
    return xc * lax.rsqrt(var + LN_EPS) * g + b


def _sigmoid(x):
    return 1.0 / (1.0 + jnp.exp(-x))


def _row_spec(tm, n):
    return pl.BlockSpec((tm, n), lambda i: (i, 0))


def _full_spec(shape):
    nd = len(shape)
    return pl.BlockSpec(shape, lambda i: (0,) * nd)


def _mm_kernel(x_ref, w_ref, o_ref):
    o_ref[...] = jnp.dot(x_ref[...].astype(BF16), w_ref[...], preferred_element_type=F32)


def _mm(x, w, tm=512):
    t, k = x.shape
    n = w.shape[1]
    return pl.pallas_call(
        _mm_kernel,
        grid=(t // tm,),
        in_specs=[_row_spec(tm, k), _full_spec((k, n))],
        out_specs=_row_spec(tm, n),
        out_shape=jax.ShapeDtypeStruct((t, n), F32),
        compiler_params=_params("parallel"),
        name="mm",
    )(x, w)


def _ln_mm_kernel(x_ref, g_ref, b_ref, w_ref, o_ref):
    y = _ln(x_ref[...], g_ref[...], b_ref[...])
    o_ref[...] = jnp.dot(y.astype(BF16), w_ref[...], preferred_element_type=F32)


def _ln_mm(x, g, b, w, tm=512):
    t, k = x.shape
    n = w.shape[1]
    return pl.pallas_call(
        _ln_mm_kernel,
        grid=(t // tm,),
        in_specs=[_row_spec(tm, k), _full_spec((1, k)), _full_spec((1, k)), _full_spec((k, n))],
        out_specs=_row_spec(tm, n),
        out_shape=jax.ShapeDtypeStruct((t, n), F32),
        compiler_params=_params("parallel"),
        name="ln_mm",
    )(x, g.reshape(1, k), b.reshape(1, k), w)


def _glu_ln_kernel(alpha, y_ref, u_ref, d_ref, wv_ref, wg_ref, h_ref, g_ref, b_ref, o_ref):
    z = jax.nn.gelu(y_ref[...] + d_ref[...] * u_ref[...]).astype(BF16)
    val = jnp.dot(z, wv_ref[...], preferred_element_type=F32)
    gate = jnp.dot(z, wg_ref[...], preferred_element_type=F32)
    mix = val * _sigmoid(gate)
    o_ref[...] = _ln(alpha * h_ref[...] + mix, g_ref[...], b_ref[...])


def _glu_ln(alpha, y, u, d, wv, wg, h, g, b, tm=512):
    t, dm = h.shape
    row = _row_spec(tm, dm)
    vec = _full_spec((1, dm))
    return pl.pallas_call(
        functools.partial(_glu_ln_kernel, alpha),
        grid=(t // tm,),
        in_specs=[row, row, vec, _full_spec((dm, dm)), _full_spec((dm, dm)), row, vec, vec],
        out_specs=row,
        out_shape=jax.ShapeDtypeStruct((t, dm), F32),
        compiler_params=_params("parallel"),
        name="glu_ln",
    )(y, u, d.reshape(1, dm), wv, wg, h, g.reshape(1, dm), b.reshape(1, dm))


def _proj_ln_kernel(alpha, o_in_ref, w_ref, h_ref, g_ref, b_ref, o_ref):
    mix = jnp.dot(o_in_ref[...].astype(BF16), w_ref[...], preferred_element_type=F32)
    o_ref[...] = _ln(alpha * h_ref[...] + mix, g_ref[...], b_ref[...])


def _proj_ln(alpha, o, w, h, g, b, tm=512):
    t, dm = h.shape
    row = _row_spec(tm, dm)
    vec = _full_spec((1, dm))
    return pl.pallas_call(
        functools.partial(_proj_ln_kernel, alpha),
        grid=(t // tm,),
        in_specs=[row, _full_spec((dm, dm)), row, vec, vec],
        out_specs=row,
        out_shape=jax.ShapeDtypeStruct((t, dm), F32),
        compiler_params=_params("parallel"),
        name="proj_ln",
    )(o, w, h, g.reshape(1, dm), b.reshape(1, dm))


def _ffn_ln_ple_kernel(alpha, h_ref, f_ref, g_ref, b_ref, wg_ref, p_ref, wp_ref, o_ref):
    h2 = _ln(alpha * h_ref[...] + f_ref[...], g_ref[...], b_ref[...])
    gate = jnp.dot(h2.astype(BF16), wg_ref[...], preferred_element_type=F32)
    emb = jnp.dot(p_ref[...].astype(BF16), wp_ref[...], preferred_element_type=F32)
    o_ref[...] = h2 + _sigmoid(gate) * emb


def _ffn_ln_ple(alpha, h, ffn, g, b, w_gate, p, w_ple, tm=512):
    t, dm = h.shape
    pd = p.shape[1]
    row = _row_spec(tm, dm)
    vec = _full_spec((1, dm))
    return pl.pallas_call(
        functools.partial(_ffn_ln_ple_kernel, alpha),
        grid=(t // tm,),
        in_specs=[row, row, vec, vec, _full_spec((dm, dm)), _row_spec(tm, pd), _full_spec((pd, dm))],
        out_specs=row,
        out_shape=jax.ShapeDtypeStruct((t, dm), F32),
        compiler_params=_params("parallel"),
        name="ffn_ln_ple",
    )(h, ffn, g.reshape(1, dm), b.reshape(1, dm), w_gate, p, w_ple)


def _s5_tables(lam_re, lam_im, log_dt, b_re, b_im, c_re, c_im):
    ell = SSM_CHUNK
    g, p = lam_re.shape
    dt = jnp.exp(log_dt)[:, None]

    def apow(k):
        mag = jnp.exp(k * lam_re * dt)
        ang = k * lam_im * dt
        return mag * jnp.cos(ang), mag * jnp.sin(ang)

    a_re, a_im = apow(1.0)
    den = lam_re * lam_re + lam_im * lam_im
    f_re = ((a_re - 1.0) * lam_re + a_im * lam_im) / den
    f_im = (a_im * lam_re - (a_re - 1.0) * lam_im) / den
    bb_re = f_re[..., None] * b_re - f_im[..., None] * b_im
    bb_im = f_re[..., None] * b_im + f_im[..., None] * b_re

    ks = jnp.arange(ell + 1, dtype=F32)[:, None, None]
    pw_re, pw_im = apow(ks)

    ca_re = c_re[None] * pw_re[:, :, None, :] - c_im[None] * pw_im[:, :, None, :]
    ca_im = c_re[None] * pw_im[:, :, None, :] + c_im[None] * pw_re[:, :, None, :]
    kk = (jnp.einsum('kgcp,gpd->kgcd', ca_re[:ell], bb_re)
          - jnp.einsum('kgcp,gpd->kgcd', ca_im[:ell], bb_im))
    jj = jnp.arange(ell)[:, None]
    tt = jnp.arange(ell)[None, :]
    lag = tt - jj
    kt = kk[jnp.clip(lag, 0, ell - 1)]
    kt = jnp.where((lag >= 0)[:, :, None, None, None], kt, 0.0)
    m = kt.transpose(2, 0, 4, 1, 3).reshape(g, ell * SSM_GROUP, ell * SSM_GROUP)

    rp_re, rp_im = pw_re[:ell][::-1], pw_im[:ell][::-1]
    ps_re = rp_re[..., None] * bb_re[None] - rp_im[..., None] * bb_im[None]
    ps_im = rp_re[..., None] * bb_im[None] + rp_im[..., None] * bb_re[None]
    pm = jnp.concatenate([ps_re, ps_im], axis=2)
    pm = pm.transpose(1, 0, 3, 2).reshape(g, ell * SSM_GROUP, 2 * p)

    q = jnp.concatenate([ca_re[1:], -ca_im[1:]], axis=3)
    q = q.transpose(1, 3, 0, 2).reshape(g, 2 * p, ell * SSM_GROUP)
    return m, pm, q


def _s5_scan_mults(lam_re, lam_im, log_dt, n_steps):
    dt = jnp.exp(log_dt)[:, None]
    ks = (SSM_CHUNK * (2.0 ** jnp.arange(n_steps, dtype=F32)))[None, :, None]
    mag = jnp.exp(ks * (lam_re * dt)[:, None, :])
    ang = ks * (lam_im * dt)[:, None, :]
    are, aim = mag * jnp.cos(ang), mag * jnp.sin(ang)
    return jnp.concatenate([are, are], axis=-1), jnp.concatenate([-aim, aim], axis=-1)


def _s5_kernel(n_chunks, n_steps, u_ref, m_ref, p_ref, q_ref, are_ref, aim_ref, y_ref):
    ub = u_ref[0].astype(BF16)
    rows = ub.shape[0]
    two_p = p_ref.shape[2]
    x = jnp.dot(ub, p_ref[0], preferred_element_type=F32)
    n_idx = lax.broadcasted_iota(jnp.int32, (rows, two_p), 0) & (n_chunks - 1)
    for s in range(n_steps):
        sh = 1 << s
        prev = jnp.where(n_idx >= sh, pltpu.roll(x, sh, 0), 0.0)
        swapped = pltpu.roll(prev, two_p // 2, 1)
        x = x + are_ref[0, s:s + 1, :] * prev + aim_ref[0, s:s + 1, :] * swapped
    x_start = jnp.where(n_idx >= 1, pltpu.roll(x, 1, 0), 0.0)
    y = jnp.dot(ub, m_ref[0], preferred_element_type=F32)
    y = y + jnp.dot(x_start.astype(BF16), q_ref[0], preferred_element_type=F32)
    y_ref[0] = y


def _s5_core(u, bsz, lam_re, lam_im, log_dt, b_re, b_im, c_re, c_im):
    t, dm = u.shape
    g = dm // SSM_GROUP
    ell = SSM_CHUNK
    seq = t // bsz
    n_chunks = seq // ell
    assert n_chunks & (n_chunks - 1) == 0, "chunks per sequence must be a power of two"
    n_steps = max(1, (n_chunks - 1).bit_length())
    m, pm, q = _s5_tables(lam_re, lam_im, log_dt, b_re, b_im, c_re, c_im)
    are, aim = _s5_scan_mults(lam_re, lam_im, log_dt, n_steps)
    rows = bsz * n_chunks
    w = ell * SSM_GROUP
    ug = u.reshape(rows, ell, g, SSM_GROUP).transpose(2, 0, 1, 3).reshape(g, rows, w)
    two_p = 2 * SSM_STATE
    yg = pl.pallas_call(
        functools.partial(_s5_kernel, n_chunks, n_steps),
        grid=(g,),
        in_specs=[
            pl.BlockSpec((1, rows, w), lambda i: (i, 0, 0)),
            pl.BlockSpec((1, w, w), lambda i: (i, 0, 0)),
            pl.BlockSpec((1, w, two_p), lambda i: (i, 0, 0)),
            pl.BlockSpec((1, two_p, w), lambda i: (i, 0, 0)),
            pl.BlockSpec((1, n_steps, two_p), lambda i: (i, 0, 0)),
            pl.BlockSpec((1, n_steps, two_p), lambda i: (i, 0, 0)),
        ],
        out_specs=pl.BlockSpec((1, rows, w), lambda i: (i, 0, 0)),
        out_shape=jax.ShapeDtypeStruct((g, rows, w), F32),
        compiler_params=_params("parallel"),
        name="s5_core",
    )(ug, m.astype(BF16), pm.astype(BF16), q.astype(BF16), are, aim)
    return yg.reshape(g, rows, ell, SSM_GROUP).transpose(1, 2, 0, 3).reshape(t, dm)


def _sb_block(qh, kb, vb, carry, tri, diag_mask):
    z = lax.dot_general(qh, kb, _NT, preferred_element_type=F32)
    soft = jnp.log1p(jnp.exp(-jnp.abs(z)))
    log_beta = jnp.minimum(z, 0.0) - soft
    log_keep = jnp.minimum(-z, 0.0) - soft
    if diag_mask is not None:
        log_keep = jnp.where(diag_mask, log_keep, 0.0)
    later = jnp.dot(log_keep.astype(BF16), tri, preferred_element_type=F32)
    w = jnp.exp(log_beta + later + carry)
    if diag_mask is not None:
        w = jnp.where(diag_mask, w, 0.0)
    out = jnp.dot(w.astype(BF16), vb, preferred_element_type=F32)
    carry = carry + jnp.sum(log_keep, axis=1, keepdims=True)
    return out, carry


def _sb_kernel(bq, scale, q_ref, k_ref, v_ref, o_ref):
    qi = pl.program_id(2)
    heads_here = q_ref.shape[2] // HEAD_DIM
    row = lax.broadcasted_iota(jnp.int32, (bq, bq), 0)
    col = lax.broadcasted_iota(jnp.int32, (bq, bq), 1)
    tri = (row > col).astype(BF16)
    diag_mask = col < row
    outs = []
    for hh in range(heads_here):
        lo = hh * HEAD_DIM
        qh = (q_ref[0, :, lo:lo + HEAD_DIM] * scale).astype(BF16)

        def load_kv(kblk, lo=lo):
            start = pl.multiple_of(kblk * bq, bq)
            kb = k_ref[0, pl.ds(start, bq), lo:lo + HEAD_DIM].astype(BF16)
            vb = v_ref[0, pl.ds(start, bq), lo:lo + HEAD_DIM].astype(BF16)
            return kb, vb

        kb, vb = load_kv(qi)
        acc, carry = _sb_block(qh, kb, vb, jnp.zeros((bq, 1), F32), tri, diag_mask)

        def body(i, state, qh=qh, load_kv=load_kv):
            acc, carry = state
            kb, vb = load_kv(qi - 1 - i)
            out, carry = _sb_block(qh, kb, vb, carry, tri, None)
            return acc + out, carry

        acc, carry = lax.fori_loop(0, qi, body, (acc, carry))
        outs.append(acc)
    o_ref[0] = jnp.concatenate(outs, axis=1)


def _stick_breaking(q, k, v, bsz, bq=128):
    t, dm = q.shape
    seq = t // bsz
    hw = LANES
    q3, k3, v3 = (a.reshape(bsz, seq, dm) for a in (q, k, v))
    out = pl.pallas_call(
        functools.partial(_sb_kernel, bq, HEAD_DIM ** -0.5),
        grid=(bsz, dm // hw, seq // bq),
        in_specs=[
            pl.BlockSpec((1, bq, hw), lambda b, h, i: (b, i, h)),
            pl.BlockSpec((1, seq, hw), lambda b, h, i: (b, 0, h)),
            pl.BlockSpec((1, seq, hw), lambda b, h, i: (b, 0, h)),
        ],
        out_specs=pl.BlockSpec((1, bq, hw), lambda b, h, i: (b, i, h)),
        out_shape=jax.ShapeDtypeStruct((bsz, seq, dm), F32),
        compiler_params=_params("parallel", "parallel", "arbitrary"),
        name="stick_breaking",
    )(q3, k3, v3)
    return out.reshape(t, dm)


def _cmpx(a, i, j):
    hi = jnp.maximum(a[i], a[j])
    lo = jnp.minimum(a[i], a[j])
    a[i], a[j] = hi, lo


def _bitonic_merge_desc(a):
    n = len(a)
    j = n // 2
    while j >= 1:
        for i in range(n):
            l = i ^ j
            if l > i:
                _cmpx(a, i, l)
        j //= 2


def _sorted_top16(s):
    k = PEER_TOPK
    a = [s[SUBLANES * i:SUBLANES * (i + 1), :] for i in range(k)]
    size = 2
    while size <= k:
        j = size // 2
        while j >= 1:
            for i in range(k):
                l = i ^ j
                if l > i:
                    if (i & size) == 0:
                        _cmpx(a, i, l)
                    else:
                        _cmpx(a, l, i)
            j //= 2
        size *= 2
    for shift in (4, 2, 1):
        b = [pltpu.roll(x, shift, 0) for x in a]
        a = [jnp.maximum(a[i], b[k - 1 - i]) for i in range(k)]
        _bitonic_merge_desc(a)
    return a


def _peer_select(v1, v2):
    k = PEER_TOPK
    cells = [v1[a] + v2[b] for a in range(k) for b in range(k) if (a + 1) * (b + 1) <= k]
    top = cells[0]
    cur = list(cells)
    remaining = jnp.full(top.shape, float(k), F32)
    tau = top
    neg = jnp.float32(-jnp.inf)
    for _ in range(k):
        m = functools.reduce(jnp.maximum, cur)
        eq = [c == m for c in cur]
        cnt = functools.reduce(jnp.add, [e.astype(F32) for e in eq])
        tau = jnp.where(remaining > 0.0, m, tau)
        remaining = remaining - cnt
        cur = [jnp.where(e, neg, c) for e, c in zip(eq, cur)]
    zsum = functools.reduce(
        jnp.add, [jnp.where(c >= tau, jnp.exp(c - top), 0.0) for c in cells])
    return tau, zsum


def _peer_kernel(n_e1, x_ref, wq_ref, k1_ref, k2_ref, u_ref, vt_ref, o_ref,
                 xb_scr, s1_scr, s2_scr, p1_scr, p2_scr, tau_scr, acc_scr):
    c = pl.program_id(1)
    tn = x_ref.shape[0]
    nk = PEER_NKEYS
    n_lt = tn // LANES

    @pl.when(c == 0)
    def _():
        xb = x_ref[...].astype(BF16)
        xb_scr[...] = xb
        q = jnp.dot(xb, wq_ref[...], preferred_element_type=F32)
        sub = lax.broadcasted_iota(jnp.int32, (SUBLANES, tn), 0)
        v1s = [jnp.zeros((SUBLANES, tn), F32) for _ in range(PEER_TOPK)]
        v2s = [jnp.zeros((SUBLANES, tn), F32) for _ in range(PEER_TOPK)]
        for h in range(PEER_HEADS):
            q1 = q[:, 2 * nk * h:2 * nk * h + nk]
            q2 = q[:, 2 * nk * h + nk:2 * nk * (h + 1)]
            s1 = lax.dot_general(k1_ref[...], q1, _NT, precision=lax.Precision.HIGHEST,
                                 preferred_element_type=F32)
            s2 = lax.dot_general(k2_ref[...], q2, _NT, precision=lax.Precision.HIGHEST,
                                 preferred_element_type=F32)
            s1_scr[h] = s1
            s2_scr[h] = s2
            t1 = _sorted_top16(s1)
            t2 = _sorted_top16(s2)
            for a in range(PEER_TOPK):
                v1s[a] = jnp.where(sub == h, t1[a], v1s[a])
                v2s[a] = jnp.where(sub == h, t2[a], v2s[a])
        tau, zsum = _peer_select(v1s, v2s)
        tau_scr[...] = tau
        inv_z = 1.0 / zsum
        for h in range(PEER_HEADS):
            m1 = v1s[0][h:h + 1, :]
            m2 = v2s[0][h:h + 1, :]
            p1_scr[h] = jnp.exp(s1_scr[h] - m1) * inv_z[h:h + 1, :]
            p2_scr[h] = jnp.exp(s2_scr[h] - m2)
        acc_scr[...] = jnp.zeros_like(acc_scr)

    act = lax.dot_general(u_ref[...], xb_scr[...], _NT, preferred_element_type=F32)
    ga = jax.nn.gelu(act)
    w_rows = []
    for e in range(n_e1):
        blk = pl.multiple_of(c * n_e1 + (e // SUBLANES) * SUBLANES, SUBLANES)
        r = e % SUBLANES
        cols = []
        for lt in range(n_lt):
            ls = slice(lt * LANES, (lt + 1) * LANES)
            s1b, p1b, taub = [], [], []
            for h in range(PEER_HEADS):
                s1_rows = s1_scr[h, pl.ds(blk, SUBLANES), ls]
                p1_rows = p1_scr[h, pl.ds(blk, SUBLANES), ls]
                s1b.append(jnp.broadcast_to(s1_rows[r:r + 1, :], (SUBLANES, LANES)))
                p1b.append(jnp.broadcast_to(p1_rows[r:r + 1, :], (SUBLANES, LANES)))
                taub.append(jnp.broadcast_to(tau_scr[h:h + 1, ls], (SUBLANES, LANES)))
            blocks = []
            for jb in range(nk // SUBLANES):
                rs = slice(jb * SUBLANES, (jb + 1) * SUBLANES)
                g = jnp.zeros((SUBLANES, LANES), F32)
                for h in range(PEER_HEADS):
                    tsum = s1b[h] + s2_scr[h, rs, ls]
                    g = g + jnp.where(tsum >= taub[h], p1b[h] * p2_scr[h, rs, ls], 0.0)
                blocks.append(g)
            cols.append(jnp.concatenate(blocks, axis=0))
        gate = jnp.concatenate(cols, axis=1) if n_lt > 1 else cols[0]
        w_rows.append((gate * ga[e * nk:(e + 1) * nk, :]).astype(BF16))
    w = jnp.concatenate(w_rows, axis=0) if n_e1 > 1 else w_rows[0]
    acc_scr[...] += jnp.dot(vt_ref[...], w, preferred_element_type=F32)

    @pl.when(c == pl.num_programs(1) - 1)
    def _():
        o_ref[...] = acc_scr[...].T


def _peer(x, wq, k1, k2, u_tab, vt_tab, tn=256, n_e1=8):
    assert n_e1 % SUBLANES == 0
    t, dm = x.shape
    n_exp = u_tab.shape[0]
    ce = n_e1 * PEER_NKEYS
    hq = wq.shape[1]
    head_scr = pltpu.VMEM((PEER_HEADS, PEER_NKEYS, tn), F32)
    return pl.pallas_call(
        functools.partial(_peer_kernel, n_e1),
        grid=(t // tn, n_exp // ce),
        in_specs=[
            pl.BlockSpec((tn, dm), lambda i, c: (i, 0)),
            pl.BlockSpec((dm, hq), lambda i, c: (0, 0)),
            pl.BlockSpec((PEER_NKEYS, PEER_NKEYS), lambda i, c: (0, 0)),
            pl.BlockSpec((PEER_NKEYS, PEER_NKEYS), lambda i, c: (0, 0)),
            pl.BlockSpec((ce, dm), lambda i, c: (c, 0)),
            pl.BlockSpec((dm, ce), lambda i, c: (0, c)),
        ],
        out_specs=pl.BlockSpec((tn, dm), lambda i, c: (i, 0)),
        out_shape=jax.ShapeDtypeStruct((t, dm), F32),
        scratch_shapes=[
            pltpu.VMEM((tn, dm), BF16),
            head_scr, head_scr, head_scr, head_scr,
            pltpu.VMEM((PEER_HEADS, tn), F32),
            pltpu.VMEM((dm, tn), F32),
        ],
        compiler_params=_params("parallel", "arbitrary"),
        name="peer",
    )(x, wq, k1, k2, u_tab, vt_tab)


@jax.jit
def _trunk(x, p, ln_mix_g, ln_mix_b, ln_ffn_g, ln_ffn_b, ssm_w_in, ssm_lam_re, ssm_lam_im,
           ssm_log_dt, ssm_b_re, ssm_b_im, ssm_c_re, ssm_c_im, ssm_d, ssm_w_glu,
           kv_ln_g, kv_ln_b, w_kv, sb_w_q, sb_w_o, peer_w_q, peer_k1, peer_k2,
           peer_u, peer_v, ple_w, ple_gate):
    bsz, seq, dm = x.shape
    depth = p.shape[0]
    n_a = ssm_w_in.shape[0]
    alpha = (2 * depth) ** 0.25
    t = bsz * seq
    h = x.reshape(t, dm)
    k_sh = v_sh = None
    for i in range(depth):
        if i < n_a:
            u = _mm(h, ssm_w_in[i].astype(BF16))
            y = _s5_core(u, bsz, ssm_lam_re[i], ssm_lam_im[i], ssm_log_dt[i],
                         ssm_b_re[i], ssm_b_im[i], ssm_c_re[i], ssm_c_im[i])
            wglu = ssm_w_glu[i].astype(BF16)
            h = _glu_ln(alpha, y, u, ssm_d[i], wglu[:, :dm], wglu[:, dm:], h,
                        ln_mix_g[i], ln_mix_b[i])
        else:
            if i == n_a:
                kv = _ln_mm(h, kv_ln_g, kv_ln_b, w_kv.astype(BF16))
                k_sh, v_sh = kv[:, :dm], kv[:, dm:]
            j = i - n_a
            q = _mm(h, sb_w_q[j].astype(BF16))
            o = _stick_breaking(q, k_sh, v_sh, bsz)
            h = _proj_ln(alpha, o, sb_w_o[j].astype(BF16), h, ln_mix_g[i], ln_mix_b[i])
        ffn = _peer(h, peer_w_q[i].astype(BF16), peer_k1[i], peer_k2[i],
                    peer_u[i].astype(BF16), peer_v[i].T.astype(BF16))
        h = _ffn_ln_ple(alpha, h, ffn, ln_ffn_g[i], ln_ffn_b[i], ple_gate[i].astype(BF16),
                        p[i].reshape(t, -1), ple_w[i].astype(BF16))
    return h.reshape(bsz, seq, dm)


def kernel(x, p, ln_mix_g, ln_mix_b, ln_ffn_g, ln_ffn_b, ssm_w_in, ssm_lam_re, ssm_lam_im, ssm_log_dt, ssm_b_re, ssm_b_im, ssm_c_re, ssm_c_im, ssm_d, ssm_w_glu, kv_ln_g, kv_ln_b, w_kv, sb_w_q, sb_w_o, peer_w_q, peer_k1, peer_k2, peer_u, peer_v, ple_w, ple_gate):
    return _trunk(x, p, ln_mix_g, ln_mix_b, ln_ffn_g, ln_ffn_b, ssm_w_in, ssm_lam_re, ssm_lam_im,
                  ssm_log_dt, ssm_b_re, ssm_b_im, ssm_c_re, ssm_c_im, ssm_d, ssm_w_glu,
                  kv_ln_g, kv_ln_b, w_kv, sb_w_q, sb_w_o, peer_w_q, peer_k1, peer_k2,
                  peer_u, peer_v, ple_w, ple_gate)
```

```python
import functools

import jax
import jax.numpy as jnp
from jax import lax
from jax.experimental import pallas as pl
from jax.experimental.pallas import tpu as pltpu

F32 = jnp.float32
BF16 = jnp.bfloat16

HEAD_DIM = 64
SSM_GROUP = 16
SSM_STATE = 64
SSM_CHUNK = 16
PEER_HEADS = 8
PEER_NKEYS = 128
PEER_TOPK = 16
LN_EPS = 1e-5

LANES = 128
SUBLANES = 8
PACKED_ROWS = 16
VMEM_LIMIT = 56 * 1024 * 1024

_NT = (((1,), (1,)), ((), ()))


def _params(*sem):
    return pltpu.CompilerParams(dimension_semantics=sem, vmem_limit_bytes=VMEM_LIMIT)


def _ln(x, g, b):
    mu = jnp.mean(x, axis=-1, keepdims=True)
    xc = x - mu
    var = jnp.mean(xc * xc, axis=-1, keepdims=True)
    return xc * lax.rsqrt(var + LN_EPS) * g + b


def _sigmoid(x):
    return 1.0 / (1.0 + jnp.exp(-x))


def _row_spec(tm, n):
    return pl.BlockSpec((tm, n), lambda i: (i, 0))


def _full_spec(shape):
    nd = len(shape)
    return pl.BlockSpec(shape, lambda i: (0,) * nd)


def _mm_kernel(x_ref, w_ref, o_ref):
    acc = jnp.dot(x_ref[...].astype(BF16), w_ref[...], preferred_element_type=F32)
    o_ref[...] = acc.astype(o_ref.dtype)


def _mm(x, w, out_dtype=F32, tm=512):
    t, k = x.shape
    n = w.shape[1]
    return pl.pallas_call(
        _mm_kernel,
        grid=(t // tm,),
        in_specs=[_row_spec(tm, k), _full_spec((k, n))],
        out_specs=_row_spec(tm, n),
        out_shape=jax.ShapeDtypeStruct((t, n), out_dtype),
        compiler_params=_params("parallel"),
        name="mm",
    )(x, w)


def _ln_mm_kernel(x_ref, g_ref, b_ref, w_ref, o_ref):
    y = _ln(x_ref[...], g_ref[...], b_ref[...])
    acc = jnp.dot(y.astype(BF16), w_ref[...], preferred_element_type=F32)
    o_ref[...] = acc.astype(o_ref.dtype)


def _ln_mm(x, g, b, w, out_dtype=F32, tm=512):
    t, k = x.shape
    n = w.shape[1]
    return pl.pallas_call(
        _ln_mm_kernel,
        grid=(t // tm,),
        in_specs=[_row_spec(tm, k), _full_spec((1, k)), _full_spec((1, k)), _full_spec((k, n))],
        out_specs=_row_spec(tm, n),
        out_shape=jax.ShapeDtypeStruct((t, n), out_dtype),
        compiler_params=_params("parallel"),
        name="ln_mm",
    )(x, g.reshape(1, k), b.reshape(1, k), w)


def _glu_ln_kernel(alpha, z_ref, wv_ref, wg_ref, h_ref, g_ref, b_ref, o_ref):
    z = z_ref[...]
    val = jnp.dot(z, wv_ref[...], preferred_element_type=F32)
    gate = jnp.dot(z, wg_ref[...], preferred_element_type=F32)
    mix = val * _sigmoid(gate)
    o_ref[...] = _ln(alpha * h_ref[...] + mix, g_ref[...], b_ref[...])


def _glu_ln(alpha, z, wv, wg, h, g, b, tm=512):
    t, dm = h.shape
    row = _row_spec(tm, dm)
    vec = _full_spec((1, dm))
    return pl.pallas_call(
        functools.partial(_glu_ln_kernel, alpha),
        grid=(t // tm,),
        in_specs=[row, _full_spec((dm, dm)), _full_spec((dm, dm)), row, vec, vec],
        out_specs=row,
        out_shape=jax.ShapeDtypeStruct((t, dm), F32),
        compiler_params=_params("parallel"),
        name="glu_ln",
    )(z, wv, wg, h, g.reshape(1, dm), b.reshape(1, dm))


def _proj_ln_kernel(alpha, o_in_ref, w_ref, h_ref, g_ref, b_ref, o_ref):
    mix = jnp.dot(o_in_ref[...], w_ref[...], preferred_element_type=F32)
    o_ref[...] = _ln(alpha * h_ref[...] + mix, g_ref[...], b_ref[...])


def _proj_ln(alpha, o, w, h, g, b, tm=512):
    t, dm = h.shape
    row = _row_spec(tm, dm)
    vec = _full_spec((1, dm))
    return pl.pallas_call(
        functools.partial(_proj_ln_kernel, alpha),
        grid=(t // tm,),
        in_specs=[row, _full_spec((dm, dm)), row, vec, vec],
        out_specs=row,
        out_shape=jax.ShapeDtypeStruct((t, dm), F32),
        compiler_params=_params("parallel"),
        name="proj_ln",
    )(o, w, h, g.reshape(1, dm), b.reshape(1, dm))


def _ffn_ln_ple_kernel(alpha, h_ref, f_ref, g_ref, b_ref, wg_ref, p_ref, wp_ref, o_ref):
    h2 = _ln(alpha * h_ref[...] + f_ref[...], g_ref[...], b_ref[...])
    gate = jnp.dot(h2.astype(BF16), wg_ref[...], preferred_element_type=F32)
    emb = jnp.dot(p_ref[...].astype(BF16), wp_ref[...], preferred_element_type=F32)
    o_ref[...] = h2 + _sigmoid(gate) * emb


def _ffn_ln_ple(alpha, h, ffn, g, b, w_gate, p, w_ple, tm=512):
    t, dm = h.shape
    pd = p.shape[1]
    row = _row_spec(tm, dm)
    vec = _full_spec((1, dm))
    return pl.pallas_call(
        functools.partial(_ffn_ln_ple_kernel, alpha),
        grid=(t // tm,),
        in_specs=[row, row, vec, vec, _full_spec((dm, dm)), _row_spec(tm, pd), _full_spec((pd, dm))],
        out_specs=row,
        out_shape=jax.ShapeDtypeStruct((t, dm), F32),
        compiler_params=_params("parallel"),
        name="ffn_ln_ple",
    )(h, ffn, g.reshape(1, dm), b.reshape(1, dm), w_gate, p, w_ple)


def _s5_tables(lam_re, lam_im, log_dt, b_re, b_im, c_re, c_im):
    ell = SSM_CHUNK
    g, p = lam_re.shape
    dt = jnp.exp(log_dt)[:, None]

    def apow(k):
        mag = jnp.exp(k * lam_re * dt)
        ang = k * lam_im * dt
        return mag * jnp.cos(ang), mag * jnp.sin(ang)

    a_re, a_im = apow(1.0)
    den = lam_re * lam_re + lam_im * lam_im
    f_re = ((a_re - 1.0) * lam_re + a_im * lam_im) / den
    f_im = (a_im * lam_re - (a_re - 1.0) * lam_im) / den
    bb_re = f_re[..., None] * b_re - f_im[..., None] * b_im
    bb_im = f_re[..., None] * b_im + f_im[..., None] * b_re

    ks = jnp.arange(ell + 1, dtype=F32)[:, None, None]
    pw_re, pw_im = apow(ks)

    ca_re = c_re[None] * pw_re[:, :, None, :] - c_im[None] * pw_im[:, :, None, :]
    ca_im = c_re[None] * pw_im[:, :, None, :] + c_im[None] * pw_re[:, :, None, :]
    kk = (jnp.einsum('kgcp,gpd->kgcd', ca_re[:ell], bb_re)
          - jnp.einsum('kgcp,gpd->kgcd', ca_im[:ell], bb_im))
    jj = jnp.arange(ell)[:, None]
    tt = jnp.arange(ell)[None, :]
    lag = tt - jj
    kt = kk[jnp.clip(lag, 0, ell - 1)]
    kt = jnp.where((lag >= 0)[:, :, None, None, None], kt, 0.0)
    m = kt.transpose(2, 0, 4, 1, 3).reshape(g, ell * SSM_GROUP, ell * SSM_GROUP)

    rp_re, rp_im = pw_re[:ell][::-1], pw_im[:ell][::-1]
    ps_re = rp_re[..., None] * bb_re[None] - rp_im[..., None] * bb_im[None]
    ps_im = rp_re[..., None] * bb_im[None] + rp_im[..., None] * bb_re[None]
    pm = jnp.stack([ps_re, ps_im], axis=2)
    pm = pm.transpose(1, 0, 4, 2, 3).reshape(g, ell * SSM_GROUP, 2, p)

    q = jnp.stack([ca_re[1:], -ca_im[1:]], axis=3)
    q = q.transpose(1, 3, 4, 0, 2).reshape(g, 2, p, ell * SSM_GROUP)
    return m, pm, q


def _s5_block_tables(lam_re, lam_im, log_dt, b_re, b_im, c_re, c_im, n_steps):
    ell, gc, p = SSM_CHUNK, SSM_GROUP, SSM_STATE
    g = lam_re.shape[0]
    gb = LANES // gc
    nb = g // gb
    m, pm, q = _s5_tables(lam_re, lam_im, log_dt, b_re, b_im, c_re, c_im)
    eye = jnp.eye(gb, dtype=F32)
    m6 = m.reshape(nb, gb, ell, gc, ell, gc)
    m_big = jnp.einsum('bgjcte,hg->bjhctge', m6, eye).reshape(nb, ell * LANES, ell * LANES)
    p6 = pm.reshape(nb, gb, ell, gc, 2, p)
    p_big = jnp.einsum('bgjcrp,hg->bjhcrgp', p6, eye).reshape(nb, ell * LANES, 2 * gb * p)
    q6 = q.reshape(nb, gb, 2, p, ell, gc)
    q_big = jnp.einsum('bgrpte,hg->brhptge', q6, eye).reshape(nb, 2 * gb * p, ell * LANES)
    dt = jnp.exp(log_dt)[:, None]
    ks = (ell * (2.0 ** jnp.arange(n_steps, dtype=F32)))[None, :, None]
    mag = jnp.exp(ks * (lam_re * dt)[:, None, :])
    ang = ks * (lam_im * dt)[:, None, :]
    are = (mag * jnp.cos(ang)).reshape(nb, gb, n_steps, p).transpose(0, 2, 1, 3).reshape(nb, n_steps, gb * p)
    aim = (mag * jnp.sin(ang)).reshape(nb, gb, n_steps, p).transpose(0, 2, 1, 3).reshape(nb, n_steps, gb * p)
    are2 = jnp.concatenate([are, are], axis=-1)
    aim2 = jnp.concatenate([-aim, aim], axis=-1)
    return m_big.astype(BF16), p_big.astype(BF16), q_big.astype(BF16), are2, aim2


def _s5_kernel(n_chunks, n_steps, u_ref, d_ref, m_ref, p_ref, q_ref, are_ref, aim_ref, z_ref):
    ell = SSM_CHUNK
    xcat = jnp.concatenate(
        [u_ref[j * n_chunks:(j + 1) * n_chunks, :].astype(BF16) for j in range(ell)], axis=1)
    x = jnp.dot(xcat, p_ref[0], preferred_element_type=F32)
    width = x.shape[1]
    n_idx = lax.broadcasted_iota(jnp.int32, x.shape, 0)
    for s in range(n_steps):
        sh = 1 << s
        prev = jnp.where(n_idx >= sh, pltpu.roll(x, sh, 0), 0.0)
        swapped = pltpu.roll(prev, width // 2, 1)
        x = x + are_ref[0, s:s + 1, :] * prev + aim_ref[0, s:s + 1, :] * swapped
    x_start = jnp.where(n_idx >= 1, pltpu.roll(x, 1, 0), 0.0)
    y = jnp.dot(xcat, m_ref[0], preferred_element_type=F32)
    y = y + jnp.dot(x_start.astype(BF16), q_ref[0], preferred_element_type=F32)
    for j in range(ell):
        rows = slice(j * n_chunks, (j + 1) * n_chunks)
        zj = jax.nn.gelu(y[:, j * LANES:(j + 1) * LANES] + d_ref[...] * u_ref[rows, :])
        z_ref[rows, :] = zj.astype(z_ref.dtype)


def _s5_gelu(u, d, bsz, lam_re, lam_im, log_dt, b_re, b_im, c_re, c_im):
    t, dm = u.shape
    seq = t // bsz
    n_chunks = seq // SSM_CHUNK
    assert n_chunks & (n_chunks - 1) == 0, "chunks per sequence must be a power of two"
    n_steps = max(1, (n_chunks - 1).bit_length())
    m_big, p_big, q_big, are2, aim2 = _s5_block_tables(
        lam_re, lam_im, log_dt, b_re, b_im, c_re, c_im, n_steps)
    nb = dm // LANES
    kw = SSM_CHUNK * LANES
    sw = p_big.shape[2]
    return pl.pallas_call(
        functools.partial(_s5_kernel, n_chunks, n_steps),
        grid=(nb, bsz),
        in_specs=[
            pl.BlockSpec((seq, LANES), lambda i, b: (b, i)),
            pl.BlockSpec((1, LANES), lambda i, b: (0, i)),
            pl.BlockSpec((1, kw, kw), lambda i, b: (i, 0, 0)),
            pl.BlockSpec((1, kw, sw), lambda i, b: (i, 0, 0)),
            pl.BlockSpec((1, sw, kw), lambda i, b: (i, 0, 0)),
            pl.BlockSpec((1, n_steps, sw), lambda i, b: (i, 0, 0)),
            pl.BlockSpec((1, n_steps, sw), lambda i, b: (i, 0, 0)),
        ],
        out_specs=pl.BlockSpec((seq, LANES), lambda i, b: (b, i)),
        out_shape=jax.ShapeDtypeStruct((t, dm), BF16),
        compiler_params=_params("parallel", "arbitrary"),
        name="s5_gelu",
    )(u, d.reshape(1, dm), m_big, p_big, q_big, are2, aim2)


def _to_chunk_order(a, bsz):
    t, f = a.shape
    n = t // (bsz * SSM_CHUNK)
    return a.reshape(bsz, n, SSM_CHUNK, f).transpose(0, 2, 1, 3).reshape(t, f)


def _from_chunk_order(a, bsz):
    t, f = a.shape
    n = t // (bsz * SSM_CHUNK)
    return a.reshape(bsz, SSM_CHUNK, n, f).transpose(0, 2, 1, 3).reshape(t, f)


def _sb_block(qh, kb, vb, carry, tri, diag_mask):
    z = lax.dot_general(qh, kb, _NT, preferred_element_type=F32)
    soft = jnp.log(1.0 + jnp.exp(-jnp.abs(z)))
    neg_log_keep = jnp.maximum(z, 0.0) + soft
    log_beta = z - neg_log_keep
    if diag_mask is not None:
        neg_log_keep = jnp.where(diag_mask, neg_log_keep, 0.0)
    later = jnp.dot(neg_log_keep.astype(BF16), tri, preferred_element_type=F32)
    w = jnp.exp(log_beta - later - carry)
    if diag_mask is not None:
        w = jnp.where(diag_mask, w, 0.0)
    out = jnp.dot(w.astype(BF16), vb, preferred_element_type=F32)
    carry = carry + jnp.sum(neg_log_keep, axis=1, keepdims=True)
    return out, carry


def _sb_kernel(bq, scale, q_ref, k_ref, v_ref, o_ref):
    qi = pl.program_id(2)
    n_heads = q_ref.shape[2] // HEAD_DIM
    row = lax.broadcasted_iota(jnp.int32, (bq, bq), 0)
    col = lax.broadcasted_iota(jnp.int32, (bq, bq), 1)
    tri = (row > col).astype(BF16)
    diag_mask = col < row
    lanes = [slice(hh * HEAD_DIM, (hh + 1) * HEAD_DIM) for hh in range(n_heads)]
    qs = [q_ref[0, :, ls] * scale for ls in lanes]

    def kv_block(kblk, ls):
        start = pl.multiple_of(kblk * bq, bq)
        return k_ref[0, pl.ds(start, bq), ls], v_ref[0, pl.ds(start, bq), ls]

    state = []
    for hh in range(n_heads):
        kb, vb = kv_block(qi, lanes[hh])
        state.extend(_sb_block(qs[hh], kb, vb, jnp.zeros((bq, 1), F32), tri, diag_mask))

    def body(i, state):
        new = []
        for hh in range(n_heads):
            acc, carry = state[2 * hh], state[2 * hh + 1]
            kb, vb = kv_block(qi - 1 - i, lanes[hh])
            out, carry = _sb_block(qs[hh], kb, vb, carry, tri, None)
            new.extend((acc + out, carry))
        return tuple(new)

    state = lax.fori_loop(0, qi, body, tuple(state))
    o_ref[0] = jnp.concatenate([state[2 * hh] for hh in range(n_heads)], axis=1).astype(o_ref.dtype)


def _stick_breaking(q, kv, bsz, bq=256, heads_per_step=4):
    t, dm = q.shape
    seq = t // bsz
    hw = heads_per_step * HEAD_DIM
    n_hb = dm // hw
    q3 = q.reshape(bsz, seq, dm)
    kv3 = kv.reshape(bsz, seq, 2 * dm)
    out = pl.pallas_call(
        functools.partial(_sb_kernel, bq, HEAD_DIM ** -0.5),
        grid=(bsz, n_hb, seq // bq),
        in_specs=[
            pl.BlockSpec((1, bq, hw), lambda b, h, i: (b, i, h)),
            pl.BlockSpec((1, seq, hw), lambda b, h, i: (b, 0, h)),
            pl.BlockSpec((1, seq, hw), lambda b, h, i: (b, 0, n_hb + h)),
        ],
        out_specs=pl.BlockSpec((1, bq, hw), lambda b, h, i: (b, i, h)),
        out_shape=jax.ShapeDtypeStruct((bsz, seq, dm), BF16),
        compiler_params=_params("parallel", "parallel", "arbitrary"),
        name="stick_breaking",
    )(q3, kv3, kv3)
    return out.reshape(t, dm)


def _cmpx(a, i, j):
    hi = jnp.maximum(a[i], a[j])
    lo = jnp.minimum(a[i], a[j])
    a[i], a[j] = hi, lo


def _bitonic_merge_desc(a):
    n = len(a)
    j = n // 2
    while j >= 1:
        for i in range(n):
            l = i ^ j
            if l > i:
                _cmpx(a, i, l)
        j //= 2


def _sorted_top16(s):
    k = PEER_TOPK
    a = [s[SUBLANES * i:SUBLANES * (i + 1), :] for i in range(k)]
    size = 2
    while size <= k:
        j = size // 2
        while j >= 1:
            for i in range(k):
                l = i ^ j
                if l > i:
                    if (i & size) == 0:
                        _cmpx(a, i, l)
                    else:
                        _cmpx(a, l, i)
            j //= 2
        size *= 2
    for shift in (4, 2, 1):
        b = [pltpu.roll(x, shift, 0) for x in a]
        a = [jnp.maximum(a[i], b[k - 1 - i]) for i in range(k)]
        _bitonic_merge_desc(a)
    return a


def _peer_select(v1, v2):
    k = PEER_TOPK
    cells = [v1[a] + v2[b] for a in range(k) for b in range(k) if (a + 1) * (b + 1) <= k]
    top = cells[0]
    cur = list(cells)
    remaining = jnp.full(top.shape, float(k), F32)
    tau = top
    neg = jnp.float32(-jnp.inf)
    for _ in range(k):
        m = functools.reduce(jnp.maximum, cur)
        eq = [c == m for c in cur]
        cnt = functools.reduce(jnp.add, [e.astype(F32) for e in eq])
        tau = jnp.where(remaining > 0.0, m, tau)
        remaining = remaining - cnt
        cur = [jnp.where(e, neg, c) for e, c in zip(eq, cur)]
    zsum = functools.reduce(
        jnp.add, [jnp.where(c >= tau, jnp.exp(c - top), 0.0) for c in cells])
    return tau, zsum


def _split_bf16(x):
    hi = x.astype(BF16)
    lo = (x - hi.astype(F32)).astype(BF16)
    return hi, lo


def _scores_t(k_hi, k_lo, qh):
    q_hi, q_lo = _split_bf16(qh)
    s = lax.dot_general(k_hi, q_hi, _NT, preferred_element_type=F32)
    s = s + lax.dot_general(k_hi, q_lo, _NT, preferred_element_type=F32)
    return s + lax.dot_general(k_lo, q_hi, _NT, preferred_element_type=F32)


def _peer_kernel(n_e1, n_chunks, x_ref, wq_ref, k1_ref, k2_ref, u_ref, vt_ref, o_ref,
                 xb_scr, k1_scr, k2_scr, ga_a, ga_b, w_a, w_b, acc_scr):
    c = pl.program_id(1)
    tn = x_ref.shape[0]
    nk = PEER_NKEYS
    n_lt = tn // LANES

    @pl.when(c == 0)
    def _():
        xb = x_ref[...].astype(BF16)
        xb_scr[...] = xb
        q = jnp.dot(xb, wq_ref[...], preferred_element_type=F32)
        k1_hi, k1_lo = _split_bf16(k1_ref[...])
        k2_hi, k2_lo = _split_bf16(k2_ref[...])
        sub = lax.broadcasted_iota(jnp.int32, (SUBLANES, tn), 0)
        v1s = [jnp.zeros((SUBLANES, tn), F32) for _ in range(PEER_TOPK)]
        v2s = [jnp.zeros((SUBLANES, tn), F32) for _ in range(PEER_TOPK)]
        s1_all, s2_all = [], []
        for h in range(PEER_HEADS):
            s1 = _scores_t(k1_hi, k1_lo, q[:, 2 * nk * h:2 * nk * h + nk])
            s2 = _scores_t(k2_hi, k2_lo, q[:, 2 * nk * h + nk:2 * nk * (h + 1)])
            s1_all.append(s1)
            s2_all.append(s2)
            t1 = _sorted_top16(s1)
            t2 = _sorted_top16(s2)
            for a in range(PEER_TOPK):
                v1s[a] = jnp.where(sub == h, t1[a], v1s[a])
                v2s[a] = jnp.where(sub == h, t2[a], v2s[a])
        tau, zsum = _peer_select(v1s, v2s)
        inv_z = 1.0 / zsum
        for h in range(PEER_HEADS):
            s1, s2 = s1_all[h], s2_all[h]
            hr = slice(h, h + 1)
            c1 = jnp.zeros_like(s1)
            r2 = jnp.zeros_like(s2)
            for b in range(PEER_TOPK):
                v2b = v2s[b][hr, :]
                c1 = c1 + (s1 + v2b >= tau[hr, :]).astype(F32)
                r2 = r2 + (v2b > s2).astype(F32)
            p1 = jnp.exp(s1 - v1s[0][hr, :]) * inv_z[hr, :]
            r2 = r2.astype(BF16)
            p2 = jnp.exp(s2 - v2s[0][hr, :]).astype(BF16)
            for lt in range(n_lt):
                ls = slice(lt * LANES, (lt + 1) * LANES)
                for kb in range(nk // SUBLANES):
                    rs = slice(kb * SUBLANES, (kb + 1) * SUBLANES)
                    k1_scr[kb, lt, h, 0] = c1[rs, ls]
                    k1_scr[kb, lt, h, 1] = p1[rs, ls]
                for jb in range(nk // PACKED_ROWS):
                    rs = slice(jb * PACKED_ROWS, (jb + 1) * PACKED_ROWS)
                    k2_scr[jb, lt, h, 0] = r2[rs, ls]
                    k2_scr[jb, lt, h, 1] = p2[rs, ls]
        acc_scr[...] = jnp.zeros_like(acc_scr)
        for ref in (ga_a, ga_b, w_a, w_b):
            ref[...] = jnp.zeros_like(ref)

    chunk1 = jnp.clip(c - 1, 0, n_chunks - 1)
    zero = jnp.zeros((PACKED_ROWS, LANES), BF16)

    def gate_rows(e, ga_old, w_new):
        blk = chunk1 * (n_e1 // SUBLANES) + e // SUBLANES
        r = e % SUBLANES
        for lt in range(n_lt):
            ls = slice(lt * LANES, (lt + 1) * LANES)
            c1b, p1b = [], []
            for h in range(PEER_HEADS):
                c1_row = k1_scr[blk, lt, h, 0][r:r + 1, :]
                p1_row = k1_scr[blk, lt, h, 1][r:r + 1, :]
                c1b.append(jnp.broadcast_to(c1_row, (PACKED_ROWS, LANES)).astype(BF16))
                p1b.append(jnp.broadcast_to(p1_row, (PACKED_ROWS, LANES)).astype(BF16))
            for jb in range(nk // PACKED_ROWS):
                g = zero
                for h in range(PEER_HEADS):
                    g = g + p1b[h] * jnp.where(k2_scr[jb, lt, h, 0] < c1b[h], k2_scr[jb, lt, h, 1], zero)
                ws = slice(e * nk + jb * PACKED_ROWS, e * nk + (jb + 1) * PACKED_ROWS)
                w_new[ws, ls] = g * ga_old[ws, ls]

    def tick(ga_new, ga_old, w_new, w_old):
        e_sub = 2
        part = None
        for s in range(n_e1 // e_sub):
            rows = slice(s * e_sub * nk, (s + 1) * e_sub * nk)
            d = jnp.dot(vt_ref[:, rows], w_old[rows, :], preferred_element_type=F32)
            part = d if part is None else part + d
            for eo in range(e_sub):
                gate_rows(s * e_sub + eo, ga_old, w_new)
            act = lax.dot_general(u_ref[rows, :], xb_scr[...], _NT, preferred_element_type=F32)
            ga_new[rows, :] = jax.nn.gelu(act).astype(BF16)
        acc_scr[...] += part

    @pl.when(c % 2 == 0)
    def _():
        tick(ga_a, ga_b, w_b, w_a)

    @pl.when(c % 2 == 1)
    def _():
        tick(ga_b, ga_a, w_a, w_b)

    @pl.when(c == pl.num_programs(1) - 1)
    def _():
        o_ref[...] = acc_scr[...].T


def _peer(x, wq, k1, k2, u_tab, vt_tab, tn=512, n_e1=8):
    assert n_e1 % SUBLANES == 0
    t, dm = x.shape
    n_exp = u_tab.shape[0]
    ce = n_e1 * PEER_NKEYS
    n_chunks = n_exp // ce
    hq = wq.shape[1]
    n_lt = tn // LANES
    k1_tab = pltpu.VMEM((PEER_NKEYS // SUBLANES, n_lt, PEER_HEADS, 2, SUBLANES, LANES), F32)
    k2_tab = pltpu.VMEM((PEER_NKEYS // PACKED_ROWS, n_lt, PEER_HEADS, 2, PACKED_ROWS, LANES), BF16)
    last = n_chunks - 1
    return pl.pallas_call(
        functools.partial(_peer_kernel, n_e1, n_chunks),
        grid=(t // tn, n_chunks + 2),
        in_specs=[
            pl.BlockSpec((tn, dm), lambda i, c: (i, 0)),
            pl.BlockSpec((dm, hq), lambda i, c: (0, 0)),
            pl.BlockSpec((PEER_NKEYS, PEER_NKEYS), lambda i, c: (0, 0)),
            pl.BlockSpec((PEER_NKEYS, PEER_NKEYS), lambda i, c: (0, 0)),
            pl.BlockSpec((ce, dm), lambda i, c: (jnp.minimum(c, last), 0)),
            pl.BlockSpec((dm, ce), lambda i, c: (0, jnp.clip(c - 2, 0, last))),
        ],
        out_specs=pl.BlockSpec((tn, dm), lambda i, c: (i, 0)),
        out_shape=jax.ShapeDtypeStruct((t, dm), F32),
        scratch_shapes=[
            pltpu.VMEM((tn, dm), BF16),
            k1_tab, k2_tab,
            pltpu.VMEM((ce, tn), BF16), pltpu.VMEM((ce, tn), BF16),
            pltpu.VMEM((ce, tn), BF16), pltpu.VMEM((ce, tn), BF16),
            pltpu.VMEM((dm, tn), F32),
        ],
        compiler_params=_params("parallel", "arbitrary"),
        name="peer",
    )(x, wq, k1, k2, u_tab, vt_tab)


@jax.jit
def _trunk(x, p, ln_mix_g, ln_mix_b, ln_ffn_g, ln_ffn_b, ssm_w_in, ssm_lam_re, ssm_lam_im,
           ssm_log_dt, ssm_b_re, ssm_b_im, ssm_c_re, ssm_c_im, ssm_d, ssm_w_glu,
           kv_ln_g, kv_ln_b, w_kv, sb_w_q, sb_w_o, peer_w_q, peer_k1, peer_k2,
           peer_u, peer_v, ple_w, ple_gate):
    bsz, seq, dm = x.shape
    depth = p.shape[0]
    n_a = ssm_w_in.shape[0]
    alpha = (2 * depth) ** 0.25
    t = bsz * seq
    h = x.reshape(t, dm)
    if n_a > 0:
        h = _to_chunk_order(h, bsz)
    kv = None
    for i in range(depth):
        p_i = p[i].reshape(t, -1)
        if i < n_a:
            p_i = _to_chunk_order(p_i, bsz)
            u = _mm(h, ssm_w_in[i].astype(BF16))
            z = _s5_gelu(u, ssm_d[i], bsz, ssm_lam_re[i], ssm_lam_im[i], ssm_log_dt[i],
                         ssm_b_re[i], ssm_b_im[i], ssm_c_re[i], ssm_c_im[i])
            wglu = ssm_w_glu[i].astype(BF16)
            h = _glu_ln(alpha, z, wglu[:, :dm], wglu[:, dm:], h, ln_mix_g[i], ln_mix_b[i])
        else:
            if i == n_a:
                kv = _ln_mm(h, kv_ln_g, kv_ln_b, w_kv.astype(BF16), out_dtype=BF16)
            j = i - n_a
            q = _mm(h, sb_w_q[j].astype(BF16), out_dtype=BF16)
            o = _stick_breaking(q, kv, bsz)
            h = _proj_ln(alpha, o, sb_w_o[j].astype(BF16), h, ln_mix_g[i], ln_mix_b[i])
        ffn = _peer(h, peer_w_q[i].astype(BF16), peer_k1[i], peer_k2[i],
                    peer_u[i].astype(BF16), peer_v[i].T.astype(BF16))
        h = _ffn_ln_ple(alpha, h, ffn, ln_ffn_g[i], ln_ffn_b[i], ple_gate[i].astype(BF16),
                        p_i, ple_w[i].astype(BF16))
        if i == n_a - 1:
            h = _from_chunk_order(h, bsz)
    return h.reshape(bsz, seq, dm)


def kernel(x, p, ln_mix_g, ln_mix_b, ln_ffn_g, ln_ffn_b, ssm_w_in, ssm_lam_re, ssm_lam_im, ssm_log_dt, ssm_b_re, ssm_b_im, ssm_c_re, ssm_c_im, ssm_d, ssm_w_glu, kv_ln_g, kv_ln_b, w_kv, sb_w_q, sb_w_o, peer_w_q, peer_k1, peer_k2, peer_u, peer_v, ple_w, ple_gate):
    return _trunk(x, p, ln_mix_g, ln_mix_b, ln_ffn_g, ln_ffn_b, ssm_w_in, ssm_lam_re, ssm_lam_im,
                  ssm_log_dt, ssm_b_re, ssm_b_im, ssm_c_re, ssm_c_im, ssm_d, ssm_w_glu,
                  kv_ln_g, kv_ln_b, w_kv, sb_w_q, sb_w_o, peer_w_q, peer_k1, peer_k2,
                  peer_u, peer_v, ple_w, ple_gate)
```

```python
import functools
import math

import jax
import jax.numpy as jnp
from jax import lax
from jax.experimental import pallas as pl
from jax.experimental.pallas import tpu as pltpu

F32 = jnp.float32
BF16 = jnp.bfloat16

HEAD_DIM = 64
SSM_GROUP = 16
SSM_STATE = 64
SSM_CHUNK = 16
PEER_HEADS = 8
PEER_NKEYS = 128
PEER_TOPK = 16
LN_EPS = 1e-5

LANES = 128
SUBLANES = 8
PACKED_ROWS = 16
VMEM_LIMIT = 56 * 1024 * 1024

_NT = (((1,), (1,)), ((), ()))


def _params(*sem, flags=None):
    return pltpu.CompilerParams(dimension_semantics=sem, vmem_limit_bytes=VMEM_LIMIT, flags=flags)


def _ln(x, g, b):
    mu = jnp.mean(x, axis=-1, keepdims=True)
    xc = x - mu
    var = jnp.mean(xc * xc, axis=-1, keepdims=True)
    return xc * lax.rsqrt(var + LN_EPS) * g + b


def _sigmoid(x):
    return 1.0 / (1.0 + jnp.exp(-x))


def _row_spec(tm, n):
    return pl.BlockSpec((tm, n), lambda i: (i, 0))


def _full_spec(shape):
    nd = len(shape)
    return pl.BlockSpec(shape, lambda i: (0,) * nd)


def _mm_kernel(x_ref, w_ref, o_ref):
    acc = jnp.dot(x_ref[...].astype(BF16), w_ref[...], preferred_element_type=F32)
    o_ref[...] = acc.astype(o_ref.dtype)


def _mm(x, w, out_dtype=F32, tm=512):
    t, k = x.shape
    n = w.shape[1]
    return pl.pallas_call(
        _mm_kernel,
        grid=(t // tm,),
        in_specs=[_row_spec(tm, k), _full_spec((k, n))],
        out_specs=_row_spec(tm, n),
        out_shape=jax.ShapeDtypeStruct((t, n), out_dtype),
        compiler_params=_params("parallel"),
        name="mm",
    )(x, w)


def _ln_mm_kernel(x_ref, g_ref, b_ref, w_ref, o_ref):
    y = _ln(x_ref[...], g_ref[...], b_ref[...])
    acc = jnp.dot(y.astype(BF16), w_ref[...], preferred_element_type=F32)
    o_ref[...] = acc.astype(o_ref.dtype)


def _ln_mm(x, g, b, w, out_dtype=F32, tm=512):
    t, k = x.shape
    n = w.shape[1]
    return pl.pallas_call(
        _ln_mm_kernel,
        grid=(t // tm,),
        in_specs=[_row_spec(tm, k), _full_spec((1, k)), _full_spec((1, k)), _full_spec((k, n))],
        out_specs=_row_spec(tm, n),
        out_shape=jax.ShapeDtypeStruct((t, n), out_dtype),
        compiler_params=_params("parallel"),
        name="ln_mm",
    )(x, g.reshape(1, k), b.reshape(1, k), w)


def _glu_ln_kernel(alpha, z_ref, wv_ref, wg_ref, h_ref, g_ref, b_ref, o_ref):
    z = z_ref[...]
    val = jnp.dot(z, wv_ref[...], preferred_element_type=F32)
    gate = jnp.dot(z, wg_ref[...], preferred_element_type=F32)
    mix = val * _sigmoid(gate)
    o_ref[...] = _ln(alpha * h_ref[...] + mix, g_ref[...], b_ref[...])


def _glu_ln(alpha, z, wv, wg, h, g, b, tm=512):
    t, dm = h.shape
    row = _row_spec(tm, dm)
    vec = _full_spec((1, dm))
    return pl.pallas_call(
        functools.partial(_glu_ln_kernel, alpha),
        grid=(t // tm,),
        in_specs=[row, _full_spec((dm, dm)), _full_spec((dm, dm)), row, vec, vec],
        out_specs=row,
        out_shape=jax.ShapeDtypeStruct((t, dm), F32),
        compiler_params=_params("parallel"),
        name="glu_ln",
    )(z, wv, wg, h, g.reshape(1, dm), b.reshape(1, dm))


def _proj_ln_kernel(alpha, o_in_ref, w_ref, h_ref, g_ref, b_ref, o_ref):
    mix = jnp.dot(o_in_ref[...], w_ref[...], preferred_element_type=F32)
    o_ref[...] = _ln(alpha * h_ref[...] + mix, g_ref[...], b_ref[...])


def _proj_ln(alpha, o, w, h, g, b, tm=512):
    t, dm = h.shape
    row = _row_spec(tm, dm)
    vec = _full_spec((1, dm))
    return pl.pallas_call(
        functools.partial(_proj_ln_kernel, alpha),
        grid=(t // tm,),
        in_specs=[row, _full_spec((dm, dm)), row, vec, vec],
        out_specs=row,
        out_shape=jax.ShapeDtypeStruct((t, dm), F32),
        compiler_params=_params("parallel"),
        name="proj_ln",
    )(o, w, h, g.reshape(1, dm), b.reshape(1, dm))


def _ffn_ln_ple_kernel(alpha, h_ref, f_ref, g_ref, b_ref, wg_ref, p_ref, wp_ref, o_ref):
    h2 = _ln(alpha * h_ref[...] + f_ref[...], g_ref[...], b_ref[...])
    gate = jnp.dot(h2.astype(BF16), wg_ref[...], preferred_element_type=F32)
    emb = jnp.dot(p_ref[...].astype(BF16), wp_ref[...], preferred_element_type=F32)
    o_ref[...] = h2 + _sigmoid(gate) * emb


def _ffn_ln_ple(alpha, h, ffn, g, b, w_gate, p, w_ple, tm=512):
    t, dm = h.shape
    pd = p.shape[1]
    row = _row_spec(tm, dm)
    vec = _full_spec((1, dm))
    return pl.pallas_call(
        functools.partial(_ffn_ln_ple_kernel, alpha),
        grid=(t // tm,),
        in_specs=[row, row, vec, vec, _full_spec((dm, dm)), _row_spec(tm, pd), _full_spec((pd, dm))],
        out_specs=row,
        out_shape=jax.ShapeDtypeStruct((t, dm), F32),
        compiler_params=_params("parallel"),
        name="ffn_ln_ple",
    )(h, ffn, g.reshape(1, dm), b.reshape(1, dm), w_gate, p, w_ple)


def _s5_tables(lam_re, lam_im, log_dt, b_re, b_im, c_re, c_im):
    ell = SSM_CHUNK
    g, p = lam_re.shape
    dt = jnp.exp(log_dt)[:, None]

    def apow(k):
        mag = jnp.exp(k * lam_re * dt)
        ang = k * lam_im * dt
        return mag * jnp.cos(ang), mag * jnp.sin(ang)

    a_re, a_im = apow(1.0)
    den = lam_re * lam_re + lam_im * lam_im
    f_re = ((a_re - 1.0) * lam_re + a_im * lam_im) / den
    f_im = (a_im * lam_re - (a_re - 1.0) * lam_im) / den
    bb_re = f_re[..., None] * b_re - f_im[..., None] * b_im
    bb_im = f_re[..., None] * b_im + f_im[..., None] * b_re

    ks = jnp.arange(ell + 1, dtype=F32)[:, None, None]
    pw_re, pw_im = apow(ks)

    ca_re = c_re[None] * pw_re[:, :, None, :] - c_im[None] * pw_im[:, :, None, :]
    ca_im = c_re[None] * pw_im[:, :, None, :] + c_im[None] * pw_re[:, :, None, :]
    kk = (jnp.einsum('kgcp,gpd->kgcd', ca_re[:ell], bb_re)
          - jnp.einsum('kgcp,gpd->kgcd', ca_im[:ell], bb_im))
    jj = jnp.arange(ell)[:, None]
    tt = jnp.arange(ell)[None, :]
    lag = tt - jj
    kt = kk[jnp.clip(lag, 0, ell - 1)]
    kt = jnp.where((lag >= 0)[:, :, None, None, None], kt, 0.0)
    m = kt.transpose(2, 0, 4, 1, 3).reshape(g, ell * SSM_GROUP, ell * SSM_GROUP)

    rp_re, rp_im = pw_re[:ell][::-1], pw_im[:ell][::-1]
    ps_re = rp_re[..., None] * bb_re[None] - rp_im[..., None] * bb_im[None]
    ps_im = rp_re[..., None] * bb_im[None] + rp_im[..., None] * bb_re[None]
    pm = jnp.stack([ps_re, ps_im], axis=2)
    pm = pm.transpose(1, 0, 4, 2, 3).reshape(g, ell * SSM_GROUP, 2, p)

    q = jnp.stack([ca_re[1:], -ca_im[1:]], axis=3)
    q = q.transpose(1, 3, 4, 0, 2).reshape(g, 2, p, ell * SSM_GROUP)
    return m, pm, q


def _s5_block_tables(lam_re, lam_im, log_dt, b_re, b_im, c_re, c_im, n_steps):
    ell, gc, p = SSM_CHUNK, SSM_GROUP, SSM_STATE
    g = lam_re.shape[0]
    gb = LANES // gc
    nb = g // gb
    m, pm, q = _s5_tables(lam_re, lam_im, log_dt, b_re, b_im, c_re, c_im)
    same = jnp.eye(gb, dtype=bool)
    zero = jnp.zeros((), BF16)
    m6 = m.astype(BF16).reshape(nb, gb, ell, gc, ell, gc).transpose(0, 2, 1, 3, 4, 5)
    m_big = jnp.where(same[None, None, :, None, None, :, None], m6[:, :, :, :, :, None, :], zero)
    m_big = m_big.reshape(nb, ell * LANES, ell * LANES)
    p6 = pm.astype(BF16).reshape(nb, gb, ell, gc, 2, p).transpose(0, 2, 1, 3, 4, 5)
    p_big = jnp.where(same[None, None, :, None, None, :, None], p6[:, :, :, :, :, None, :], zero)
    p_big = p_big.reshape(nb, ell * LANES, 2 * gb * p)
    q6 = q.astype(BF16).reshape(nb, gb, 2, p, ell, gc).transpose(0, 2, 1, 3, 4, 5)
    q_big = jnp.where(same[None, None, :, None, None, :, None], q6[:, :, :, :, :, None, :], zero)
    q_big = q_big.reshape(nb, 2 * gb * p, ell * LANES)
    dt = jnp.exp(log_dt)[:, None]
    ks = (ell * (2.0 ** jnp.arange(n_steps, dtype=F32)))[None, :, None]
    mag = jnp.exp(ks * (lam_re * dt)[:, None, :])
    ang = ks * (lam_im * dt)[:, None, :]
    are = (mag * jnp.cos(ang)).reshape(nb, gb, n_steps, p).transpose(0, 2, 1, 3).reshape(nb, n_steps, gb * p)
    aim = (mag * jnp.sin(ang)).reshape(nb, gb, n_steps, p).transpose(0, 2, 1, 3).reshape(nb, n_steps, gb * p)
    are2 = jnp.concatenate([are, are], axis=-1)
    aim2 = jnp.concatenate([-aim, aim], axis=-1)
    return m_big.astype(BF16), p_big.astype(BF16), q_big.astype(BF16), are2, aim2


def _s5_kernel(n_chunks, n_steps, u_ref, d_ref, m_ref, p_ref, q_ref, are_ref, aim_ref, z_ref):
    ell = SSM_CHUNK
    xcat = jnp.concatenate(
        [u_ref[j * n_chunks:(j + 1) * n_chunks, :].astype(BF16) for j in range(ell)], axis=1)
    x = jnp.dot(xcat, p_ref[0], preferred_element_type=F32)
    width = x.shape[1]
    n_idx = lax.broadcasted_iota(jnp.int32, x.shape, 0)
    for s in range(n_steps):
        sh = 1 << s
        prev = jnp.where(n_idx >= sh, pltpu.roll(x, sh, 0), 0.0)
        swapped = pltpu.roll(prev, width // 2, 1)
        x = x + are_ref[0, s:s + 1, :] * prev + aim_ref[0, s:s + 1, :] * swapped
    x_start = jnp.where(n_idx >= 1, pltpu.roll(x, 1, 0), 0.0)
    y = jnp.dot(xcat, m_ref[0], preferred_element_type=F32)
    y = y + jnp.dot(x_start.astype(BF16), q_ref[0], preferred_element_type=F32)
    for j in range(ell):
        rows = slice(j * n_chunks, (j + 1) * n_chunks)
        zj = jax.nn.gelu(y[:, j * LANES:(j + 1) * LANES] + d_ref[...] * u_ref[rows, :])
        z_ref[rows, :] = zj.astype(z_ref.dtype)


def _s5_gelu(u, d, bsz, lam_re, lam_im, log_dt, b_re, b_im, c_re, c_im):
    t, dm = u.shape
    seq = t // bsz
    n_chunks = seq // SSM_CHUNK
    assert n_chunks & (n_chunks - 1) == 0, "chunks per sequence must be a power of two"
    n_steps = max(1, (n_chunks - 1).bit_length())
    m_big, p_big, q_big, are2, aim2 = _s5_block_tables(
        lam_re, lam_im, log_dt, b_re, b_im, c_re, c_im, n_steps)
    nb = dm // LANES
    kw = SSM_CHUNK * LANES
    sw = p_big.shape[2]
    return pl.pallas_call(
        functools.partial(_s5_kernel, n_chunks, n_steps),
        grid=(nb, bsz),
        in_specs=[
            pl.BlockSpec((seq, LANES), lambda i, b: (b, i)),
            pl.BlockSpec((1, LANES), lambda i, b: (0, i)),
            pl.BlockSpec((1, kw, kw), lambda i, b: (i, 0, 0)),
            pl.BlockSpec((1, kw, sw), lambda i, b: (i, 0, 0)),
            pl.BlockSpec((1, sw, kw), lambda i, b: (i, 0, 0)),
            pl.BlockSpec((1, n_steps, sw), lambda i, b: (i, 0, 0)),
            pl.BlockSpec((1, n_steps, sw), lambda i, b: (i, 0, 0)),
        ],
        out_specs=pl.BlockSpec((seq, LANES), lambda i, b: (b, i)),
        out_shape=jax.ShapeDtypeStruct((t, dm), BF16),
        compiler_params=_params("parallel", "arbitrary"),
        name="s5_gelu",
    )(u, d.reshape(1, dm), m_big, p_big, q_big, are2, aim2)


def _to_chunk_order(a, bsz):
    t, f = a.shape
    n = t // (bsz * SSM_CHUNK)
    return a.reshape(bsz, n, SSM_CHUNK, f).transpose(0, 2, 1, 3).reshape(t, f)


def _from_chunk_order(a, bsz):
    t, f = a.shape
    n = t // (bsz * SSM_CHUNK)
    return a.reshape(bsz, SSM_CHUNK, n, f).transpose(0, 2, 1, 3).reshape(t, f)


def _sb_block(qh, kb, vb, carry, tri, diag_mask):
    z2 = lax.dot_general(qh, kb, _NT, preferred_element_type=F32)
    pos = jnp.maximum(z2, 0.0)
    neg = z2 - pos
    soft = jnp.log2(1.0 + jnp.exp2(neg - pos))
    neg_log_keep = pos + soft
    log_beta = neg - soft
    if diag_mask is not None:
        neg_log_keep = jnp.where(diag_mask, neg_log_keep, 0.0)
    later = jnp.dot(neg_log_keep.astype(BF16), tri, preferred_element_type=F32)
    w = jnp.exp2(log_beta - later - carry)
    if diag_mask is not None:
        w = jnp.where(diag_mask, w, 0.0)
    out = jnp.dot(w.astype(BF16), vb, preferred_element_type=F32)
    carry = carry + jnp.sum(neg_log_keep, axis=1, keepdims=True)
    return out, carry


def _sb_kernel(bq, scale, q_ref, k_ref, v_ref, o_ref):
    qi = pl.program_id(2)
    n_heads = q_ref.shape[2] // HEAD_DIM
    row = lax.broadcasted_iota(jnp.int32, (bq, bq), 0)
    col = lax.broadcasted_iota(jnp.int32, (bq, bq), 1)
    tri = (row > col).astype(BF16)
    diag_mask = col < row
    lanes = [slice(hh * HEAD_DIM, (hh + 1) * HEAD_DIM) for hh in range(n_heads)]
    qs = [(q_ref[0, :, ls].astype(F32) * scale).astype(BF16) for ls in lanes]

    def kv_block(kblk, ls):
        start = pl.multiple_of(kblk * bq, bq)
        return k_ref[0, pl.ds(start, bq), ls], v_ref[0, pl.ds(start, bq), ls]

    state = []
    for hh in range(n_heads):
        kb, vb = kv_block(qi, lanes[hh])
        state.extend(_sb_block(qs[hh], kb, vb, jnp.zeros((bq, 1), F32), tri, diag_mask))

    def body(i, state):
        new = []
        for hh in range(n_heads):
            acc, carry = state[2 * hh], state[2 * hh + 1]
            kb, vb = kv_block(qi - 1 - i, lanes[hh])
            out, carry = _sb_block(qs[hh], kb, vb, carry, tri, None)
            new.extend((acc + out, carry))
        return tuple(new)

    state = lax.fori_loop(0, qi, body, tuple(state))
    o_ref[0] = jnp.concatenate([state[2 * hh] for hh in range(n_heads)], axis=1).astype(o_ref.dtype)


def _stick_breaking(q, kv, bsz, bq=256, heads_per_step=8):
    t, dm = q.shape
    seq = t // bsz
    hw = heads_per_step * HEAD_DIM
    n_hb = dm // hw
    q3 = q.reshape(bsz, seq, dm)
    kv3 = kv.reshape(bsz, seq, 2 * dm)
    out = pl.pallas_call(
        functools.partial(_sb_kernel, bq, HEAD_DIM ** -0.5 * math.log2(math.e)),
        grid=(bsz, n_hb, seq // bq),
        in_specs=[
            pl.BlockSpec((1, bq, hw), lambda b, h, i: (b, i, h)),
            pl.BlockSpec((1, seq, hw), lambda b, h, i: (b, 0, h)),
            pl.BlockSpec((1, seq, hw), lambda b, h, i: (b, 0, n_hb + h)),
        ],
        out_specs=pl.BlockSpec((1, bq, hw), lambda b, h, i: (b, i, h)),
        out_shape=jax.ShapeDtypeStruct((bsz, seq, dm), BF16),
        compiler_params=_params("parallel", "parallel", "arbitrary"),
        name="stick_breaking",
    )(q3, kv3, kv3)
    return out.reshape(t, dm)


def _cmpx(a, i, j):
    hi = jnp.maximum(a[i], a[j])
    lo = jnp.minimum(a[i], a[j])
    a[i], a[j] = hi, lo


def _bitonic_merge_desc(a):
    n = len(a)
    j = n // 2
    while j >= 1:
        for i in range(n):
            l = i ^ j
            if l > i:
                _cmpx(a, i, l)
        j //= 2


def _sorted_top16(s):
    k = PEER_TOPK
    a = [s[SUBLANES * i:SUBLANES * (i + 1), :] for i in range(k)]
    size = 2
    while size <= k:
        j = size // 2
        while j >= 1:
            for i in range(k):
                l = i ^ j
                if l > i:
                    if (i & size) == 0:
                        _cmpx(a, i, l)
                    else:
                        _cmpx(a, l, i)
            j //= 2
        size *= 2
    for shift in (4, 2, 1):
        b = [pltpu.roll(x, shift, 0) for x in a]
        a = [jnp.maximum(a[i], b[k - 1 - i]) for i in range(k)]
        _bitonic_merge_desc(a)
    return a


def _peer_select(v1, v2):
    k = PEER_TOPK
    cells = [v1[a] + v2[b] for a in range(k) for b in range(k) if (a + 1) * (b + 1) <= k]
    top = cells[0]
    cur = list(cells)
    remaining = jnp.full(top.shape, float(k), F32)
    tau = top
    neg = jnp.float32(-jnp.inf)
    for _ in range(k):
        m = functools.reduce(jnp.maximum, cur)
        eq = [c == m for c in cur]
        cnt = functools.reduce(jnp.add, [e.astype(F32) for e in eq])
        tau = jnp.where(remaining > 0.0, m, tau)
        remaining = remaining - cnt
        cur = [jnp.where(e, neg, c) for e, c in zip(eq, cur)]
    zsum = functools.reduce(
        jnp.add, [jnp.where(c >= tau, jnp.exp(c - top), 0.0) for c in cells])
    return tau, zsum


def _prefix_count(vals, test):
    m8 = test(vals[7])
    m4 = test(jnp.where(m8, vals[11], vals[3]))
    m2 = test(jnp.where(m8, jnp.where(m4, vals[13], vals[9]), jnp.where(m4, vals[5], vals[1])))
    hi = jnp.where(m4, jnp.where(m2, vals[14], vals[12]), jnp.where(m2, vals[10], vals[8]))
    lo = jnp.where(m4, jnp.where(m2, vals[6], vals[4]), jnp.where(m2, vals[2], vals[0]))
    m1 = test(jnp.where(m8, hi, lo))
    m16 = test(vals[15])
    count = jnp.where(m8, 8.0, 0.0) + jnp.where(m4, 4.0, 0.0) + jnp.where(m2, 2.0, 0.0)
    return count + jnp.where(m1, 1.0, 0.0) + jnp.where(m16, 1.0, 0.0)


def _split_bf16(x):
    hi = x.astype(BF16)
    lo = (x - hi.astype(F32)).astype(BF16)
    return hi, lo


def _scores_t(k_hi, k_lo, qh):
    q_hi, q_lo = _split_bf16(qh)
    s = lax.dot_general(k_hi, q_hi, _NT, preferred_element_type=F32)
    s = s + lax.dot_general(k_hi, q_lo, _NT, preferred_element_type=F32)
    return s + lax.dot_general(k_lo, q_hi, _NT, preferred_element_type=F32)


def _peer_kernel(n_e1, n_chunks, x_ref, wq_ref, k1_ref, k2_ref, u_ref, vt_ref, o_ref,
                 xb_scr, k1_scr, k2_scr, ga_a, ga_b, w_a, w_b, acc_scr):
    c = pl.program_id(1)
    tn = x_ref.shape[0]
    nk = PEER_NKEYS
    n_lt = tn // LANES

    @pl.when(c == 0)
    def _():
        xb = x_ref[...].astype(BF16)
        xb_scr[...] = xb
        q = jnp.dot(xb, wq_ref[...], preferred_element_type=F32)
        k1_hi, k1_lo = _split_bf16(k1_ref[...])
        k2_hi, k2_lo = _split_bf16(k2_ref[...])
        sub = lax.broadcasted_iota(jnp.int32, (SUBLANES, tn), 0)
        v1s = [jnp.zeros((SUBLANES, tn), F32) for _ in range(PEER_TOPK)]
        v2s = [jnp.zeros((SUBLANES, tn), F32) for _ in range(PEER_TOPK)]
        s1_all, s2_all = [], []
        for h in range(PEER_HEADS):
            s1 = _scores_t(k1_hi, k1_lo, q[:, 2 * nk * h:2 * nk * h + nk])
            s2 = _scores_t(k2_hi, k2_lo, q[:, 2 * nk * h + nk:2 * nk * (h + 1)])
            s1_all.append(s1)
            s2_all.append(s2)
            t1 = _sorted_top16(s1)
            t2 = _sorted_top16(s2)
            for a in range(PEER_TOPK):
                v1s[a] = jnp.where(sub == h, t1[a], v1s[a])
                v2s[a] = jnp.where(sub == h, t2[a], v2s[a])
        tau, zsum = _peer_select(v1s, v2s)
        inv_z = 1.0 / zsum
        for h in range(PEER_HEADS):
            s1, s2 = s1_all[h], s2_all[h]
            hr = slice(h, h + 1)
            v2_rows = [v2s[b][hr, :] for b in range(PEER_TOPK)]
            tau_h = tau[hr, :]
            c1 = _prefix_count(v2_rows, lambda row: s1 + row >= tau_h)
            r2 = _prefix_count(v2_rows, lambda row: row > s2)
            p1 = jnp.exp(s1 - v1s[0][hr, :]) * inv_z[hr, :]
            r2 = r2.astype(BF16)
            p2 = jnp.exp(s2 - v2s[0][hr, :]).astype(BF16)
            for lt in range(n_lt):
                ls = slice(lt * LANES, (lt + 1) * LANES)
                for kb in range(nk // SUBLANES):
                    rs = slice(kb * SUBLANES, (kb + 1) * SUBLANES)
                    k1_scr[kb, lt, h, 0] = c1[rs, ls]
                    k1_scr[kb, lt, h, 1] = p1[rs, ls]
                for jb in range(nk // PACKED_ROWS):
                    rs = slice(jb * PACKED_ROWS, (jb + 1) * PACKED_ROWS)
                    k2_scr[jb, lt, h, 0] = r2[rs, ls]
                    k2_scr[jb, lt, h, 1] = p2[rs, ls]
        acc_scr[...] = jnp.zeros_like(acc_scr)
        for ref in (ga_a, ga_b, w_a, w_b):
            ref[...] = jnp.zeros_like(ref)

    chunk1 = jnp.clip(c - 1, 0, n_chunks - 1)
    zero = jnp.zeros((PACKED_ROWS, LANES), BF16)

    def gate_rows(e, ga_old, w_new):
        blk = chunk1 * (n_e1 // SUBLANES) + e // SUBLANES
        r = e % SUBLANES
        for lt in range(n_lt):
            ls = slice(lt * LANES, (lt + 1) * LANES)
            c1b, p1b = [], []
            for h in range(PEER_HEADS):
                c1_row = k1_scr[blk, lt, h, 0][r:r + 1, :]
                p1_row = k1_scr[blk, lt, h, 1][r:r + 1, :]
                c1b.append(jnp.broadcast_to(c1_row, (PACKED_ROWS, LANES)).astype(BF16))
                p1b.append(jnp.broadcast_to(p1_row, (PACKED_ROWS, LANES)).astype(BF16))
            for jb in range(nk // PACKED_ROWS):
                g = zero
                for h in range(PEER_HEADS):
                    g = g + p1b[h] * jnp.where(k2_scr[jb, lt, h, 0] < c1b[h], k2_scr[jb, lt, h, 1], zero)
                ws = slice(e * nk + jb * PACKED_ROWS, e * nk + (jb + 1) * PACKED_ROWS)
                w_new[ws, ls] = g * ga_old[ws, ls]

    def tick(ga_new, ga_old, w_new, w_old):
        e_sub = 2
        part = None
        for s in range(n_e1 // e_sub):
            rows = slice(s * e_sub * nk, (s + 1) * e_sub * nk)
            d = jnp.dot(vt_ref[:, rows], w_old[rows, :], preferred_element_type=F32)
            part = d if part is None else part + d
            for eo in range(e_sub):
                gate_rows(s * e_sub + eo, ga_old, w_new)
            act = lax.dot_general(u_ref[rows, :], xb_scr[...], _NT, preferred_element_type=F32)
            ga_new[rows, :] = jax.nn.gelu(act.astype(BF16))
        acc_scr[...] += part

    @pl.when(c % 2 == 0)
    def _():
        tick(ga_a, ga_b, w_b, w_a)

    @pl.when(c % 2 == 1)
    def _():
        tick(ga_b, ga_a, w_a, w_b)

    @pl.when(c == pl.num_programs(1) - 1)
    def _():
        o_ref[...] = acc_scr[...].T


def _peer(x, wq, k1, k2, u_tab, vt_tab, tn=512, n_e1=8):
    assert n_e1 % SUBLANES == 0
    t, dm = x.shape
    n_exp = u_tab.shape[0]
    ce = n_e1 * PEER_NKEYS
    n_chunks = n_exp // ce
    hq = wq.shape[1]
    n_lt = tn // LANES
    k1_tab = pltpu.VMEM((PEER_NKEYS // SUBLANES, n_lt, PEER_HEADS, 2, SUBLANES, LANES), F32)
    k2_tab = pltpu.VMEM((PEER_NKEYS // PACKED_ROWS, n_lt, PEER_HEADS, 2, PACKED_ROWS, LANES), BF16)
    last = n_chunks - 1
    return pl.pallas_call(
        functools.partial(_peer_kernel, n_e1, n_chunks),
        grid=(t // tn, n_chunks + 2),
        in_specs=[
            pl.BlockSpec((tn, dm), lambda i, c: (i, 0)),
            pl.BlockSpec((dm, hq), lambda i, c: (0, 0)),
            pl.BlockSpec((PEER_NKEYS, PEER_NKEYS), lambda i, c: (0, 0)),
            pl.BlockSpec((PEER_NKEYS, PEER_NKEYS), lambda i, c: (0, 0)),
            pl.BlockSpec((ce, dm), lambda i, c: (jnp.minimum(c, last), 0)),
            pl.BlockSpec((dm, ce), lambda i, c: (0, jnp.clip(c - 2, 0, last))),
        ],
        out_specs=pl.BlockSpec((tn, dm), lambda i, c: (i, 0)),
        out_shape=jax.ShapeDtypeStruct((t, dm), F32),
        scratch_shapes=[
            pltpu.VMEM((tn, dm), BF16),
            k1_tab, k2_tab,
            pltpu.VMEM((ce, tn), BF16), pltpu.VMEM((ce, tn), BF16),
            pltpu.VMEM((ce, tn), BF16), pltpu.VMEM((ce, tn), BF16),
            pltpu.VMEM((dm, tn), F32),
        ],
        compiler_params=_params("parallel", "arbitrary"),
        name="peer",
    )(x, wq, k1, k2, u_tab, vt_tab)


@jax.jit
def _trunk(x, p, ln_mix_g, ln_mix_b, ln_ffn_g, ln_ffn_b, ssm_w_in, ssm_lam_re, ssm_lam_im,
           ssm_log_dt, ssm_b_re, ssm_b_im, ssm_c_re, ssm_c_im, ssm_d, ssm_w_glu,
           kv_ln_g, kv_ln_b, w_kv, sb_w_q, sb_w_o, peer_w_q, peer_k1, peer_k2,
           peer_u, peer_v, ple_w, ple_gate):
    bsz, seq, dm = x.shape
    depth = p.shape[0]
    n_a = ssm_w_in.shape[0]
    alpha = (2 * depth) ** 0.25
    t = bsz * seq
    h = x.reshape(t, dm)
    if n_a > 0:
        h = _to_chunk_order(h, bsz)
    kv = None
    for i in range(depth):
        p_i = p[i].reshape(t, -1)
        if i < n_a:
            p_i = _to_chunk_order(p_i, bsz)
            u = _mm(h, ssm_w_in[i].astype(BF16))
            z = _s5_gelu(u, ssm_d[i], bsz, ssm_lam_re[i], ssm_lam_im[i], ssm_log_dt[i],
                         ssm_b_re[i], ssm_b_im[i], ssm_c_re[i], ssm_c_im[i])
            wglu = ssm_w_glu[i].astype(BF16)
            h = _glu_ln(alpha, z, wglu[:, :dm], wglu[:, dm:], h, ln_mix_g[i], ln_mix_b[i])
        else:
            if i == n_a:
                kv = _ln_mm(h, kv_ln_g, kv_ln_b, w_kv.astype(BF16), out_dtype=BF16)
            j = i - n_a
            q = _mm(h, sb_w_q[j].astype(BF16), out_dtype=BF16)
            o = _stick_breaking(q, kv, bsz)
            h = _proj_ln(alpha, o, sb_w_o[j].astype(BF16), h, ln_mix_g[i], ln_mix_b[i])
        ffn = _peer(h, peer_w_q[i].astype(BF16), peer_k1[i], peer_k2[i],
                    peer_u[i].astype(BF16), peer_v[i].T.astype(BF16))
        h = _ffn_ln_ple(alpha, h, ffn, ln_ffn_g[i], ln_ffn_b[i], ple_gate[i].astype(BF16),
                        p_i, ple_w[i].astype(BF16))
        if i == n_a - 1:
            h = _from_chunk_order(h, bsz)
    return h.reshape(bsz, seq, dm)


def kernel(x, p, ln_mix_g, ln_mix_b, ln_ffn_g, ln_ffn_b, ssm_w_in, ssm_lam_re, ssm_lam_im, ssm_log_dt, ssm_b_re, ssm_b_im, ssm_c_re, ssm_c_im, ssm_d, ssm_w_glu, kv_ln_g, kv_ln_b, w_kv, sb_w_q, sb_w_o, peer_w_q, peer_k1, peer_k2, peer_u, peer_v, ple_w, ple_gate):
    return _trunk(x, p, ln_mix_g, ln_mix_b, ln_ffn_g, ln_ffn_b, ssm_w_in, ssm_lam_re, ssm_lam_im,
                  ssm_log_dt, ssm_b_re, ssm_b_im, ssm_c_re, ssm_c_im, ssm_d, ssm_w_glu,
                  kv_ln_g, kv_ln_b, w_kv, sb_w_q, sb_w_o, peer_w_q, peer_k1, peer_k2,
                  peer_u, peer_v, ple_w, ple_gate)
```

```python
import functools
import math

import jax
import jax.numpy as jnp
from jax import lax
from jax.experimental import pallas as pl
from jax.experimental.pallas import tpu as pltpu

F32 = jnp.float32
BF16 = jnp.bfloat16

HEAD_DIM = 64
SSM_GROUP = 16
SSM_STATE = 64
SSM_CHUNK = 16
PEER_HEADS = 8
PEER_NKEYS = 128
PEER_TOPK = 16
LN_EPS = 1e-5

LANES = 128
SUBLANES = 8
PACKED_ROWS = 16
VMEM_LIMIT = 56 * 1024 * 1024

_NT = (((1,), (1,)), ((), ()))


def _params(*sem, flags=None):
    return pltpu.CompilerParams(dimension_semantics=sem, vmem_limit_bytes=VMEM_LIMIT, flags=flags)


def _ln(x, g, b):
    mu = jnp.mean(x, axis=-1, keepdims=True)
    xc = x - mu
    var = jnp.mean(xc * xc, axis=-1, keepdims=True)
    return xc * lax.rsqrt(var + LN_EPS) * g + b


def _sigmoid(x):
    return 1.0 / (1.0 + jnp.exp(-x))


def _row_spec(tm, n):
    return pl.BlockSpec((tm, n), lambda i: (i, 0))


def _full_spec(shape):
    nd = len(shape)
    return pl.BlockSpec(shape, lambda i: (0,) * nd)


def _mm_kernel(x_ref, w_ref, o_ref):
    acc = jnp.dot(x_ref[...].astype(BF16), w_ref[...], preferred_element_type=F32)
    o_ref[...] = acc.astype(o_ref.dtype)


def _mm(x, w, out_dtype=F32, tm=512):
    t, k = x.shape
    n = w.shape[1]
    return pl.pallas_call(
        _mm_kernel,
        grid=(t // tm,),
        in_specs=[_row_spec(tm, k), _full_spec((k, n))],
        out_specs=_row_spec(tm, n),
        out_shape=jax.ShapeDtypeStruct((t, n), out_dtype),
        compiler_params=_params("parallel"),
        name="mm",
    )(x, w)


def _ln_mm_kernel(x_ref, g_ref, b_ref, w_ref, o_ref):
    y = _ln(x_ref[...], g_ref[...], b_ref[...])
    acc = jnp.dot(y.astype(BF16), w_ref[...], preferred_element_type=F32)
    o_ref[...] = acc.astype(o_ref.dtype)


def _ln_mm(x, g, b, w, out_dtype=F32, tm=512):
    t, k = x.shape
    n = w.shape[1]
    return pl.pallas_call(
        _ln_mm_kernel,
        grid=(t // tm,),
        in_specs=[_row_spec(tm, k), _full_spec((1, k)), _full_spec((1, k)), _full_spec((k, n))],
        out_specs=_row_spec(tm, n),
        out_shape=jax.ShapeDtypeStruct((t, n), out_dtype),
        compiler_params=_params("parallel"),
        name="ln_mm",
    )(x, g.reshape(1, k), b.reshape(1, k), w)


def _glu_ln_kernel(alpha, z_ref, wv_ref, wg_ref, h_ref, g_ref, b_ref, o_ref):
    z = z_ref[...]
    val = jnp.dot(z, wv_ref[...], preferred_element_type=F32)
    gate = jnp.dot(z, wg_ref[...], preferred_element_type=F32)
    mix = val * _sigmoid(gate)
    o_ref[...] = _ln(alpha * h_ref[...] + mix, g_ref[...], b_ref[...])


def _glu_ln(alpha, z, wv, wg, h, g, b, tm=512):
    t, dm = h.shape
    row = _row_spec(tm, dm)
    vec = _full_spec((1, dm))
    return pl.pallas_call(
        functools.partial(_glu_ln_kernel, alpha),
        grid=(t // tm,),
        in_specs=[row, _full_spec((dm, dm)), _full_spec((dm, dm)), row, vec, vec],
        out_specs=row,
        out_shape=jax.ShapeDtypeStruct((t, dm), F32),
        compiler_params=_params("parallel"),
        name="glu_ln",
    )(z, wv, wg, h, g.reshape(1, dm), b.reshape(1, dm))


def _proj_ln_kernel(alpha, o_in_ref, w_ref, h_ref, g_ref, b_ref, o_ref):
    mix = jnp.dot(o_in_ref[...], w_ref[...], preferred_element_type=F32)
    o_ref[...] = _ln(alpha * h_ref[...] + mix, g_ref[...], b_ref[...])


def _proj_ln(alpha, o, w, h, g, b, tm=512):
    t, dm = h.shape
    row = _row_spec(tm, dm)
    vec = _full_spec((1, dm))
    return pl.pallas_call(
        functools.partial(_proj_ln_kernel, alpha),
        grid=(t // tm,),
        in_specs=[row, _full_spec((dm, dm)), row, vec, vec],
        out_specs=row,
        out_shape=jax.ShapeDtypeStruct((t, dm), F32),
        compiler_params=_params("parallel"),
        name="proj_ln",
    )(o, w, h, g.reshape(1, dm), b.reshape(1, dm))


def _ffn_ln_ple_kernel(alpha, h_ref, f_ref, g_ref, b_ref, wg_ref, p_ref, wp_ref, o_ref):
    h2 = _ln(alpha * h_ref[...] + f_ref[...], g_ref[...], b_ref[...])
    gate = jnp.dot(h2.astype(BF16), wg_ref[...], preferred_element_type=F32)
    emb = jnp.dot(p_ref[...].astype(BF16), wp_ref[...], preferred_element_type=F32)
    o_ref[...] = h2 + _sigmoid(gate) * emb


def _ffn_ln_ple(alpha, h, ffn, g, b, w_gate, p, w_ple, tm=512):
    t, dm = h.shape
    pd = p.shape[1]
    row = _row_spec(tm, dm)
    vec = _full_spec((1, dm))
    return pl.pallas_call(
        functools.partial(_ffn_ln_ple_kernel, alpha),
        grid=(t // tm,),
        in_specs=[row, row, vec, vec, _full_spec((dm, dm)), _row_spec(tm, pd), _full_spec((pd, dm))],
        out_specs=row,
        out_shape=jax.ShapeDtypeStruct((t, dm), F32),
        compiler_params=_params("parallel"),
        name="ffn_ln_ple",
    )(h, ffn, g.reshape(1, dm), b.reshape(1, dm), w_gate, p, w_ple)


def _s5_tables(lam_re, lam_im, log_dt, b_re, b_im, c_re, c_im):
    ell = SSM_CHUNK
    g, p = lam_re.shape
    dt = jnp.exp(log_dt)[:, None]

    def apow(k):
        mag = jnp.exp(k * lam_re * dt)
        ang = k * lam_im * dt
        return mag * jnp.cos(ang), mag * jnp.sin(ang)

    a_re, a_im = apow(1.0)
    den = lam_re * lam_re + lam_im * lam_im
    f_re = ((a_re - 1.0) * lam_re + a_im * lam_im) / den
    f_im = (a_im * lam_re - (a_re - 1.0) * lam_im) / den
    bb_re = f_re[..., None] * b_re - f_im[..., None] * b_im
    bb_im = f_re[..., None] * b_im + f_im[..., None] * b_re

    ks = jnp.arange(ell + 1, dtype=F32)[:, None, None]
    pw_re, pw_im = apow(ks)

    ca_re = c_re[None] * pw_re[:, :, None, :] - c_im[None] * pw_im[:, :, None, :]
    ca_im = c_re[None] * pw_im[:, :, None, :] + c_im[None] * pw_re[:, :, None, :]
    kk = (jnp.einsum('kgcp,gpd->kgcd', ca_re[:ell], bb_re)
          - jnp.einsum('kgcp,gpd->kgcd', ca_im[:ell], bb_im))

    rp_re, rp_im = pw_re[:ell][::-1], pw_im[:ell][::-1]
    ps_re = rp_re[..., None] * bb_re[None] - rp_im[..., None] * bb_im[None]
    ps_im = rp_re[..., None] * bb_im[None] + rp_im[..., None] * bb_re[None]
    pm = jnp.stack([ps_re, ps_im], axis=2)
    pm = pm.transpose(1, 0, 4, 2, 3).reshape(g, ell * SSM_GROUP, 2, p)

    q = jnp.stack([ca_re[1:], -ca_im[1:]], axis=3)
    q = q.transpose(1, 3, 4, 0, 2).reshape(g, 2, p, ell * SSM_GROUP)
    return kk, pm, q


def _s5_block_tables(lam_re, lam_im, log_dt, b_re, b_im, c_re, c_im, n_steps):
    ell, gc, p = SSM_CHUNK, SSM_GROUP, SSM_STATE
    g = lam_re.shape[0]
    gb = LANES // gc
    nb = g // gb
    kk, pm, q = _s5_tables(lam_re, lam_im, log_dt, b_re, b_im, c_re, c_im)
    same = jnp.eye(gb, dtype=bool)
    zero = jnp.zeros((), BF16)
    k6 = kk.astype(BF16).reshape(ell, nb, gb, gc, gc).transpose(1, 0, 2, 4, 3)
    k_blk = jnp.where(same[None, None, :, None, :, None], k6[:, :, :, :, None, :], zero)
    k_blk = k_blk.reshape(nb, ell, LANES, LANES)
    kz = jnp.concatenate([jnp.zeros((nb, 1, LANES, LANES), BF16), k_blk], axis=1)
    half = ell // 2
    top = jnp.concatenate([kz[:, 1::2], kz[:, 2::2]], axis=-1)
    bottom = jnp.concatenate([kz[:, 0::2][:, :half], kz[:, 1::2]], axis=-1)
    w_tab = jnp.concatenate([top, bottom], axis=-2)
    p6 = pm.astype(BF16).reshape(nb, gb, ell, gc, 2, p).transpose(0, 2, 1, 3, 4, 5)
    p_big = jnp.where(same[None, None, :, None, None, :, None], p6[:, :, :, :, :, None, :], zero)
    p_big = p_big.reshape(nb, ell * LANES, 2 * gb * p)
    q6 = q.astype(BF16).reshape(nb, gb, 2, p, ell, gc).transpose(0, 2, 1, 3, 4, 5)
    q_big = jnp.where(same[None, None, :, None, None, :, None], q6[:, :, :, :, :, None, :], zero)
    q_big = q_big.reshape(nb, 2 * gb * p, ell * LANES)
    dt = jnp.exp(log_dt)[:, None]
    ks = (ell * (2.0 ** jnp.arange(n_steps, dtype=F32)))[None, :, None]
    mag = jnp.exp(ks * (lam_re * dt)[:, None, :])
    ang = ks * (lam_im * dt)[:, None, :]
    are = (mag * jnp.cos(ang)).reshape(nb, gb, n_steps, p).transpose(0, 2, 1, 3).reshape(nb, n_steps, gb * p)
    aim = (mag * jnp.sin(ang)).reshape(nb, gb, n_steps, p).transpose(0, 2, 1, 3).reshape(nb, n_steps, gb * p)
    are2 = jnp.concatenate([are, are], axis=-1)
    aim2 = jnp.concatenate([-aim, aim], axis=-1)
    return w_tab, p_big, q_big, are2, aim2


def _s5_kernel(n_chunks, n_steps, u_ref, d_ref, w_ref, p_ref, q_ref, are_ref, aim_ref, z_ref):
    ell = SSM_CHUNK
    half = ell // 2
    xs = [jnp.concatenate([u_ref[j * n_chunks:(j + 1) * n_chunks, :].astype(BF16)
                           for j in (2 * jp, 2 * jp + 1)], axis=1) for jp in range(half)]
    xcat = jnp.concatenate(xs, axis=1)
    x = jnp.dot(xcat, p_ref[0], preferred_element_type=F32)
    width = x.shape[1]
    n_idx = lax.broadcasted_iota(jnp.int32, x.shape, 0)
    for s in range(n_steps):
        sh = 1 << s
        prev = jnp.where(n_idx >= sh, pltpu.roll(x, sh, 0), 0.0)
        swapped = pltpu.roll(prev, width // 2, 1)
        x = x + are_ref[0, s:s + 1, :] * prev + aim_ref[0, s:s + 1, :] * swapped
    x_start = jnp.where(n_idx >= 1, pltpu.roll(x, 1, 0), 0.0)
    y_state = jnp.dot(x_start.astype(BF16), q_ref[0], preferred_element_type=F32)
    for tp in range(half):
        y = y_state[:, 2 * tp * LANES:2 * (tp + 1) * LANES]
        for jp in range(tp + 1):
            y = y + jnp.dot(xs[jp], w_ref[0, tp - jp], preferred_element_type=F32)
        for tl in range(2):
            rows = slice((2 * tp + tl) * n_chunks, (2 * tp + tl + 1) * n_chunks)
            zj = jax.nn.gelu(y[:, tl * LANES:(tl + 1) * LANES] + d_ref[...] * u_ref[rows, :])
            z_ref[rows, :] = zj.astype(z_ref.dtype)


def _s5_gelu(u, d, bsz, lam_re, lam_im, log_dt, b_re, b_im, c_re, c_im):
    t, dm = u.shape
    seq = t // bsz
    n_chunks = seq // SSM_CHUNK
    assert n_chunks & (n_chunks - 1) == 0, "chunks per sequence must be a power of two"
    n_steps = max(1, (n_chunks - 1).bit_length())
    w_tab, p_big, q_big, are2, aim2 = _s5_block_tables(
        lam_re, lam_im, log_dt, b_re, b_im, c_re, c_im, n_steps)
    nb = dm // LANES
    kw = SSM_CHUNK * LANES
    sw = p_big.shape[2]
    half = SSM_CHUNK // 2
    return pl.pallas_call(
        functools.partial(_s5_kernel, n_chunks, n_steps),
        grid=(nb, bsz),
        in_specs=[
            pl.BlockSpec((seq, LANES), lambda i, b: (b, i)),
            pl.BlockSpec((1, LANES), lambda i, b: (0, i)),
            pl.BlockSpec((1, half, 2 * LANES, 2 * LANES), lambda i, b: (i, 0, 0, 0)),
            pl.BlockSpec((1, kw, sw), lambda i, b: (i, 0, 0)),
            pl.BlockSpec((1, sw, kw), lambda i, b: (i, 0, 0)),
            pl.BlockSpec((1, n_steps, sw), lambda i, b: (i, 0, 0)),
            pl.BlockSpec((1, n_steps, sw), lambda i, b: (i, 0, 0)),
        ],
        out_specs=pl.BlockSpec((seq, LANES), lambda i, b: (b, i)),
        out_shape=jax.ShapeDtypeStruct((t, dm), BF16),
        compiler_params=_params("parallel", "arbitrary"),
        name="s5_gelu",
    )(u, d.reshape(1, dm), w_tab, p_big, q_big, are2, aim2)


def _to_chunk_order(a, bsz):
    t, f = a.shape
    n = t // (bsz * SSM_CHUNK)
    return a.reshape(bsz, n, SSM_CHUNK, f).transpose(0, 2, 1, 3).reshape(t, f)


def _from_chunk_order(a, bsz):
    t, f = a.shape
    n = t // (bsz * SSM_CHUNK)
    return a.reshape(bsz, SSM_CHUNK, n, f).transpose(0, 2, 1, 3).reshape(t, f)


def _sb_scores(qh, kb, tri, diag_mask):
    z2 = lax.dot_general(qh, kb, _NT, preferred_element_type=F32)
    pos = jnp.maximum(z2, 0.0)
    neg = z2 - pos
    soft = jnp.log2(1.0 + jnp.exp2(neg - pos))
    neg_log_keep = pos + soft
    log_beta = neg - soft
    if diag_mask is not None:
        neg_log_keep = jnp.where(diag_mask, neg_log_keep, 0.0)
    later = jnp.dot(neg_log_keep.astype(BF16), tri, preferred_element_type=F32)
    arg = log_beta - later
    if diag_mask is not None:
        arg = jnp.where(diag_mask, arg, -jnp.inf)
    return arg, jnp.sum(neg_log_keep, axis=1, keepdims=True)


def _sb_kernel(bq, scale, q_ref, k_ref, v_ref, o_ref, arg_scr, acc_scr):
    qi = pl.program_id(2)
    n_heads = q_ref.shape[2] // HEAD_DIM
    row = lax.broadcasted_iota(jnp.int32, (bq, bq), 0)
    col = lax.broadcasted_iota(jnp.int32, (bq, bq), 1)
    tri = (row > col).astype(BF16)
    diag_mask = col < row
    lanes = [slice(hh * HEAD_DIM, (hh + 1) * HEAD_DIM) for hh in range(n_heads)]
    qs = [(q_ref[0, :, ls].astype(F32) * scale).astype(BF16) for ls in lanes]

    def rows_of(ref, kblk, ls):
        return ref[0, pl.ds(pl.multiple_of(kblk * bq, bq), bq), ls]

    def finish(hh, carry, vblk):
        w = jnp.exp2(arg_scr[hh] - carry)
        return jnp.dot(w.astype(BF16), rows_of(v_ref, vblk, lanes[hh]), preferred_element_type=F32)

    acc_scr[...] = jnp.zeros_like(acc_scr)
    state = []
    for hh in range(n_heads):
        arg, rowsum = _sb_scores(qs[hh], rows_of(k_ref, qi, lanes[hh]), tri, diag_mask)
        arg_scr[hh] = arg
        state.extend((jnp.zeros((bq, 1), F32), rowsum))

    def body(i, state):
        new = []
        for hh in range(n_heads):
            carry, rowsum = state[2 * hh], state[2 * hh + 1]
            acc_scr[:, lanes[hh]] += finish(hh, carry, qi - i)
            arg, next_rowsum = _sb_scores(qs[hh], rows_of(k_ref, qi - 1 - i, lanes[hh]), tri, None)
            arg_scr[hh] = arg
            new.extend((carry + rowsum, next_rowsum))
        return tuple(new)

    state = lax.fori_loop(0, qi, body, tuple(state))
    outs = [acc_scr[:, lanes[hh]] + finish(hh, state[2 * hh], 0) for hh in range(n_heads)]
    o_ref[0] = jnp.concatenate(outs, axis=1).astype(o_ref.dtype)


def _stick_breaking(q, kv, bsz, bq=256, heads_per_step=8):
    t, dm = q.shape
    seq = t // bsz
    hw = heads_per_step * HEAD_DIM
    n_hb = dm // hw
    q3 = q.reshape(bsz, seq, dm)
    kv3 = kv.reshape(bsz, seq, 2 * dm)
    out = pl.pallas_call(
        functools.partial(_sb_kernel, bq, HEAD_DIM ** -0.5 * math.log2(math.e)),
        grid=(bsz, n_hb, seq // bq),
        in_specs=[
            pl.BlockSpec((1, bq, hw), lambda b, h, i: (b, i, h)),
            pl.BlockSpec((1, seq, hw), lambda b, h, i: (b, 0, h)),
            pl.BlockSpec((1, seq, hw), lambda b, h, i: (b, 0, n_hb + h)),
        ],
        out_specs=pl.BlockSpec((1, bq, hw), lambda b, h, i: (b, i, h)),
        out_shape=jax.ShapeDtypeStruct((bsz, seq, dm), BF16),
        scratch_shapes=[pltpu.VMEM((heads_per_step, bq, bq), F32), pltpu.VMEM((bq, hw), F32)],
        compiler_params=_params("parallel", "parallel", "arbitrary"),
        name="stick_breaking",
    )(q3, kv3, kv3)
    return out.reshape(t, dm)


def _cmpx(a, i, j):
    hi = jnp.maximum(a[i], a[j])
    lo = jnp.minimum(a[i], a[j])
    a[i], a[j] = hi, lo


def _bitonic_merge_desc(a):
    n = len(a)
    j = n // 2
    while j >= 1:
        for i in range(n):
            l = i ^ j
            if l > i:
                _cmpx(a, i, l)
        j //= 2


def _sorted_top16(s):
    k = PEER_TOPK
    a = [s[SUBLANES * i:SUBLANES * (i + 1), :] for i in range(k)]
    size = 2
    while size <= k:
        j = size // 2
        while j >= 1:
            for i in range(k):
                l = i ^ j
                if l > i:
                    if (i & size) == 0:
                        _cmpx(a, i, l)
                    else:
                        _cmpx(a, l, i)
            j //= 2
        size *= 2
    for shift in (4, 2, 1):
        b = [pltpu.roll(x, shift, 0) for x in a]
        a = [jnp.maximum(a[i], b[k - 1 - i]) for i in range(k)]
        _bitonic_merge_desc(a)
    return a


def _peer_select(v1, v2):
    k = PEER_TOPK
    cells = [v1[a] + v2[b] for a in range(k) for b in range(k) if (a + 1) * (b + 1) <= k]
    top = cells[0]
    cur = list(cells)
    remaining = jnp.full(top.shape, float(k), F32)
    tau = top
    neg = jnp.float32(-jnp.inf)
    for _ in range(k):
        m = functools.reduce(jnp.maximum, cur)
        eq = [c == m for c in cur]
        cnt = functools.reduce(jnp.add, [e.astype(F32) for e in eq])
        tau = jnp.where(remaining > 0.0, m, tau)
        remaining = remaining - cnt
        cur = [jnp.where(e, neg, c) for e, c in zip(eq, cur)]
    zsum = functools.reduce(
        jnp.add, [jnp.where(c >= tau, jnp.exp(c - top), 0.0) for c in cells])
    return tau, zsum


def _prefix_count(vals, test):
    m8 = test(vals[7])
    m4 = test(jnp.where(m8, vals[11], vals[3]))
    m2 = test(jnp.where(m8, jnp.where(m4, vals[13], vals[9]), jnp.where(m4, vals[5], vals[1])))
    hi = jnp.where(m4, jnp.where(m2, vals[14], vals[12]), jnp.where(m2, vals[10], vals[8]))
    lo = jnp.where(m4, jnp.where(m2, vals[6], vals[4]), jnp.where(m2, vals[2], vals[0]))
    m1 = test(jnp.where(m8, hi, lo))
    m16 = test(vals[15])
    count = jnp.where(m8, 8.0, 0.0) + jnp.where(m4, 4.0, 0.0) + jnp.where(m2, 2.0, 0.0)
    return count + jnp.where(m1, 1.0, 0.0) + jnp.where(m16, 1.0, 0.0)


def _split_bf16(x):
    hi = x.astype(BF16)
    lo = (x - hi.astype(F32)).astype(BF16)
    return hi, lo


def _scores_t(k_hi, k_lo, qh):
    q_hi, q_lo = _split_bf16(qh)
    s = lax.dot_general(k_hi, q_hi, _NT, preferred_element_type=F32)
    s = s + lax.dot_general(k_hi, q_lo, _NT, preferred_element_type=F32)
    return s + lax.dot_general(k_lo, q_hi, _NT, preferred_element_type=F32)


def _peer_kernel(n_e1, n_chunks, x_ref, wq_ref, k1_ref, k2_ref, u_ref, vt_ref, o_ref,
                 xb_scr, k1_scr, k2_scr, ga_a, ga_b, w_a, w_b, acc_scr):
    c = pl.program_id(1)
    tn = x_ref.shape[0]
    nk = PEER_NKEYS
    n_lt = tn // LANES

    @pl.when(c == 0)
    def _():
        xb = x_ref[...].astype(BF16)
        xb_scr[...] = xb
        q = jnp.dot(xb, wq_ref[...], preferred_element_type=F32)
        k1_hi, k1_lo = _split_bf16(k1_ref[...])
        k2_hi, k2_lo = _split_bf16(k2_ref[...])
        sub = lax.broadcasted_iota(jnp.int32, (SUBLANES, tn), 0)
        v1s = [jnp.zeros((SUBLANES, tn), F32) for _ in range(PEER_TOPK)]
        v2s = [jnp.zeros((SUBLANES, tn), F32) for _ in range(PEER_TOPK)]
        s1_all, s2_all = [], []
        for h in range(PEER_HEADS):
            s1 = _scores_t(k1_hi, k1_lo, q[:, 2 * nk * h:2 * nk * h + nk])
            s2 = _scores_t(k2_hi, k2_lo, q[:, 2 * nk * h + nk:2 * nk * (h + 1)])
            s1_all.append(s1)
            s2_all.append(s2)
            t1 = _sorted_top16(s1)
            t2 = _sorted_top16(s2)
            for a in range(PEER_TOPK):
                v1s[a] = jnp.where(sub == h, t1[a], v1s[a])
                v2s[a] = jnp.where(sub == h, t2[a], v2s[a])
        tau, zsum = _peer_select(v1s, v2s)
        inv_z = 1.0 / zsum
        for h in range(PEER_HEADS):
            s1, s2 = s1_all[h], s2_all[h]
            hr = slice(h, h + 1)
            v2_rows = [v2s[b][hr, :] for b in range(PEER_TOPK)]
            tau_h = tau[hr, :]
            c1 = _prefix_count(v2_rows, lambda row: s1 + row >= tau_h)
            r2 = _prefix_count(v2_rows, lambda row: row > s2)
            p1 = jnp.exp(s1 - v1s[0][hr, :]) * inv_z[hr, :]
            r2 = r2.astype(BF16)
            p2 = jnp.exp(s2 - v2s[0][hr, :]).astype(BF16)
            for lt in range(n_lt):
                ls = slice(lt * LANES, (lt + 1) * LANES)
                for kb in range(nk // SUBLANES):
                    rs = slice(kb * SUBLANES, (kb + 1) * SUBLANES)
                    k1_scr[kb, lt, h, 0] = c1[rs, ls]
                    k1_scr[kb, lt, h, 1] = p1[rs, ls]
                for jb in range(nk // PACKED_ROWS):
                    rs = slice(jb * PACKED_ROWS, (jb + 1) * PACKED_ROWS)
                    k2_scr[jb, lt, h, 0] = r2[rs, ls]
                    k2_scr[jb, lt, h, 1] = p2[rs, ls]
        acc_scr[...] = jnp.zeros_like(acc_scr)
        for ref in (ga_a, ga_b, w_a, w_b):
            ref[...] = jnp.zeros_like(ref)

    chunk1 = jnp.clip(c - 1, 0, n_chunks - 1)
    zero = jnp.zeros((PACKED_ROWS, LANES), BF16)

    def gate_rows(e, ga_old, w_new):
        blk = chunk1 * (n_e1 // SUBLANES) + e // SUBLANES
        r = e % SUBLANES
        for lt in range(n_lt):
            ls = slice(lt * LANES, (lt + 1) * LANES)
            c1b, p1b = [], []
            for h in range(PEER_HEADS):
                c1_row = k1_scr[blk, lt, h, 0][r:r + 1, :]
                p1_row = k1_scr[blk, lt, h, 1][r:r + 1, :]
                c1b.append(jnp.broadcast_to(c1_row, (PACKED_ROWS, LANES)).astype(BF16))
                p1b.append(jnp.broadcast_to(p1_row, (PACKED_ROWS, LANES)).astype(BF16))
            for jb in range(nk // PACKED_ROWS):
                g = zero
                for h in range(PEER_HEADS):
                    g = g + p1b[h] * jnp.where(k2_scr[jb, lt, h, 0] < c1b[h], k2_scr[jb, lt, h, 1], zero)
                ws = slice(e * nk + jb * PACKED_ROWS, e * nk + (jb + 1) * PACKED_ROWS)
                w_new[ws, ls] = g * ga_old[ws, ls]

    def tick(ga_new, ga_old, w_new, w_old):
        e_sub = 2
        part = None
        for s in range(n_e1 // e_sub):
            rows = slice(s * e_sub * nk, (s + 1) * e_sub * nk)
            d = jnp.dot(vt_ref[:, rows], w_old[rows, :], preferred_element_type=F32)
            part = d if part is None else part + d
            for eo in range(e_sub):
                gate_rows(s * e_sub + eo, ga_old, w_new)
            act = lax.dot_general(u_ref[rows, :], xb_scr[...], _NT, preferred_element_type=F32)
            ga_new[rows, :] = jax.nn.gelu(act.astype(BF16))
        acc_scr[...] += part

    @pl.when(c % 2 == 0)
    def _():
        tick(ga_a, ga_b, w_b, w_a)

    @pl.when(c % 2 == 1)
    def _():
        tick(ga_b, ga_a, w_a, w_b)

    @pl.when(c == pl.num_programs(1) - 1)
    def _():
        o_ref[...] = acc_scr[...].T


def _peer(x, wq, k1, k2, u_tab, vt_tab, tn=512, n_e1=8):
    assert n_e1 % SUBLANES == 0
    t, dm = x.shape
    n_exp = u_tab.shape[0]
    ce = n_e1 * PEER_NKEYS
    n_chunks = n_exp // ce
    hq = wq.shape[1]
    n_lt = tn // LANES
    k1_tab = pltpu.VMEM((PEER_NKEYS // SUBLANES, n_lt, PEER_HEADS, 2, SUBLANES, LANES), F32)
    k2_tab = pltpu.VMEM((PEER_NKEYS // PACKED_ROWS, n_lt, PEER_HEADS, 2, PACKED_ROWS, LANES), BF16)
    last = n_chunks - 1
    return pl.pallas_call(
        functools.partial(_peer_kernel, n_e1, n_chunks),
        grid=(t // tn, n_chunks + 2),
        in_specs=[
            pl.BlockSpec((tn, dm), lambda i, c: (i, 0)),
            pl.BlockSpec((dm, hq), lambda i, c: (0, 0)),
            pl.BlockSpec((PEER_NKEYS, PEER_NKEYS), lambda i, c: (0, 0)),
            pl.BlockSpec((PEER_NKEYS, PEER_NKEYS), lambda i, c: (0, 0)),
            pl.BlockSpec((ce, dm), lambda i, c: (jnp.minimum(c, last), 0)),
            pl.BlockSpec((dm, ce), lambda i, c: (0, jnp.clip(c - 2, 0, last))),
        ],
        out_specs=pl.BlockSpec((tn, dm), lambda i, c: (i, 0)),
        out_shape=jax.ShapeDtypeStruct((t, dm), F32),
        scratch_shapes=[
            pltpu.VMEM((tn, dm), BF16),
            k1_tab, k2_tab,
            pltpu.VMEM((ce, tn), BF16), pltpu.VMEM((ce, tn), BF16),
            pltpu.VMEM((ce, tn), BF16), pltpu.VMEM((ce, tn), BF16),
            pltpu.VMEM((dm, tn), F32),
        ],
        compiler_params=_params("parallel", "arbitrary"),
        name="peer",
    )(x, wq, k1, k2, u_tab, vt_tab)


@jax.jit
def _trunk(x, p, ln_mix_g, ln_mix_b, ln_ffn_g, ln_ffn_b, ssm_w_in, ssm_lam_re, ssm_lam_im,
           ssm_log_dt, ssm_b_re, ssm_b_im, ssm_c_re, ssm_c_im, ssm_d, ssm_w_glu,
           kv_ln_g, kv_ln_b, w_kv, sb_w_q, sb_w_o, peer_w_q, peer_k1, peer_k2,
           peer_u, peer_v, ple_w, ple_gate):
    bsz, seq, dm = x.shape
    depth = p.shape[0]
    n_a = ssm_w_in.shape[0]
    alpha = (2 * depth) ** 0.25
    t = bsz * seq
    h = x.reshape(t, dm)
    if n_a > 0:
        h = _to_chunk_order(h, bsz)
    kv = None
    for i in range(depth):
        p_i = p[i].reshape(t, -1)
        if i < n_a:
            p_i = _to_chunk_order(p_i, bsz)
            u = _mm(h, ssm_w_in[i].astype(BF16))
            z = _s5_gelu(u, ssm_d[i], bsz, ssm_lam_re[i], ssm_lam_im[i], ssm_log_dt[i],
                         ssm_b_re[i], ssm_b_im[i], ssm_c_re[i], ssm_c_im[i])
            wglu = ssm_w_glu[i].astype(BF16)
            h = _glu_ln(alpha, z, wglu[:, :dm], wglu[:, dm:], h, ln_mix_g[i], ln_mix_b[i])
        else:
            if i == n_a:
                kv = _ln_mm(h, kv_ln_g, kv_ln_b, w_kv.astype(BF16), out_dtype=BF16)
            j = i - n_a
            q = _mm(h, sb_w_q[j].astype(BF16), out_dtype=BF16)
            o = _stick_breaking(q, kv, bsz)
            h = _proj_ln(alpha, o, sb_w_o[j].astype(BF16), h, ln_mix_g[i], ln_mix_b[i])
        ffn = _peer(h, peer_w_q[i].astype(BF16), peer_k1[i], peer_k2[i],
                    peer_u[i].astype(BF16), peer_v[i].T.astype(BF16))
        h = _ffn_ln_ple(alpha, h, ffn, ln_ffn_g[i], ln_ffn_b[i], ple_gate[i].astype(BF16),
                        p_i, ple_w[i].astype(BF16))
        if i == n_a - 1:
            h = _from_chunk_order(h, bsz)
    return h.reshape(bsz, seq, dm)


def kernel(x, p, ln_mix_g, ln_mix_b, ln_ffn_g, ln_ffn_b, ssm_w_in, ssm_lam_re, ssm_lam_im, ssm_log_dt, ssm_b_re, ssm_b_im, ssm_c_re, ssm_c_im, ssm_d, ssm_w_glu, kv_ln_g, kv_ln_b, w_kv, sb_w_q, sb_w_o, peer_w_q, peer_k1, peer_k2, peer_u, peer_v, ple_w, ple_gate):
    return _trunk(x, p, ln_mix_g, ln_mix_b, ln_ffn_g, ln_ffn_b, ssm_w_in, ssm_lam_re, ssm_lam_im,
                  ssm_log_dt, ssm_b_re, ssm_b_im, ssm_c_re, ssm_c_im, ssm_d, ssm_w_glu,
                  kv_ln_g, kv_ln_b, w_kv, sb_w_q, sb_w_o, peer_w_q, peer_k1, peer_k2,
                  peer_u, peer_v, ple_w, ple_gate)
```

```python
import functools
import math

import jax
import jax.numpy as jnp
from jax import lax
from jax.experimental import pallas as pl
from jax.experimental.pallas import tpu as pltpu

F32 = jnp.float32
BF16 = jnp.bfloat16

HEAD_DIM = 64
SSM_GROUP = 16
SSM_STATE = 64
SSM_CHUNK = 16
PEER_HEADS = 8
PEER_NKEYS = 128
PEER_TOPK = 16
LN_EPS = 1e-5

LANES = 128
SUBLANES = 8
PACKED_ROWS = 16
VMEM_LIMIT = 56 * 1024 * 1024

_NT = (((1,), (1,)), ((), ()))


def _params(*sem, flags=None):
    return pltpu.CompilerParams(dimension_semantics=sem, vmem_limit_bytes=VMEM_LIMIT, flags=flags)


def _ln(x, g, b):
    mu = jnp.mean(x, axis=-1, keepdims=True)
    xc = x - mu
    var = jnp.mean(xc * xc, axis=-1, keepdims=True)
    return xc * lax.rsqrt(var + LN_EPS) * g + b


def _sigmoid(x):
    return 1.0 / (1.0 + jnp.exp(-x))


def _row_spec(tm, n):
    return pl.BlockSpec((tm, n), lambda i: (i, 0))


def _full_spec(shape):
    nd = len(shape)
    return pl.BlockSpec(shape, lambda i: (0,) * nd)


def _mm_kernel(x_ref, w_ref, o_ref):
    acc = jnp.dot(x_ref[...].astype(BF16), w_ref[...], preferred_element_type=F32)
    o_ref[...] = acc.astype(o_ref.dtype)


def _mm(x, w, out_dtype=F32, tm=512):
    t, k = x.shape
    n = w.shape[1]
    return pl.pallas_call(
        _mm_kernel,
        grid=(t // tm,),
        in_specs=[_row_spec(tm, k), _full_spec((k, n))],
        out_specs=_row_spec(tm, n),
        out_shape=jax.ShapeDtypeStruct((t, n), out_dtype),
        compiler_params=_params("parallel"),
        name="mm",
    )(x, w)


def _ln_mm_kernel(x_ref, g_ref, b_ref, w_ref, o_ref):
    y = _ln(x_ref[...], g_ref[...], b_ref[...])
    acc = jnp.dot(y.astype(BF16), w_ref[...], preferred_element_type=F32)
    o_ref[...] = acc.astype(o_ref.dtype)


def _ln_mm(x, g, b, w, out_dtype=F32, tm=512):
    t, k = x.shape
    n = w.shape[1]
    return pl.pallas_call(
        _ln_mm_kernel,
        grid=(t // tm,),
        in_specs=[_row_spec(tm, k), _full_spec((1, k)), _full_spec((1, k)), _full_spec((k, n))],
        out_specs=_row_spec(tm, n),
        out_shape=jax.ShapeDtypeStruct((t, n), out_dtype),
        compiler_params=_params("parallel"),
        name="ln_mm",
    )(x, g.reshape(1, k), b.reshape(1, k), w)


def _glu_ln_kernel(alpha, z_ref, wv_ref, wg_ref, h_ref, g_ref, b_ref, o_ref):
    z = z_ref[...]
    val = jnp.dot(z, wv_ref[...], preferred_element_type=F32)
    gate = jnp.dot(z, wg_ref[...], preferred_element_type=F32)
    mix = val * _sigmoid(gate)
    o_ref[...] = _ln(alpha * h_ref[...] + mix, g_ref[...], b_ref[...])


def _glu_ln(alpha, z, wv, wg, h, g, b, tm=512):
    t, dm = h.shape
    row = _row_spec(tm, dm)
    vec = _full_spec((1, dm))
    return pl.pallas_call(
        functools.partial(_glu_ln_kernel, alpha),
        grid=(t // tm,),
        in_specs=[row, _full_spec((dm, dm)), _full_spec((dm, dm)), row, vec, vec],
        out_specs=row,
        out_shape=jax.ShapeDtypeStruct((t, dm), F32),
        compiler_params=_params("parallel"),
        name="glu_ln",
    )(z, wv, wg, h, g.reshape(1, dm), b.reshape(1, dm))


def _proj_ln_kernel(alpha, o_in_ref, w_ref, h_ref, g_ref, b_ref, o_ref):
    mix = jnp.dot(o_in_ref[...], w_ref[...], preferred_element_type=F32)
    o_ref[...] = _ln(alpha * h_ref[...] + mix, g_ref[...], b_ref[...])


def _proj_ln(alpha, o, w, h, g, b, tm=512):
    t, dm = h.shape
    row = _row_spec(tm, dm)
    vec = _full_spec((1, dm))
    return pl.pallas_call(
        functools.partial(_proj_ln_kernel, alpha),
        grid=(t // tm,),
        in_specs=[row, _full_spec((dm, dm)), row, vec, vec],
        out_specs=row,
        out_shape=jax.ShapeDtypeStruct((t, dm), F32),
        compiler_params=_params("parallel"),
        name="proj_ln",
    )(o, w, h, g.reshape(1, dm), b.reshape(1, dm))


def _ffn_ln_ple_kernel(alpha, h_ref, f_ref, g_ref, b_ref, wg_ref, p_ref, wp_ref, o_ref):
    h2 = _ln(alpha * h_ref[...] + f_ref[...], g_ref[...], b_ref[...])
    gate = jnp.dot(h2.astype(BF16), wg_ref[...], preferred_element_type=F32)
    emb = jnp.dot(p_ref[...].astype(BF16), wp_ref[...], preferred_element_type=F32)
    o_ref[...] = h2 + _sigmoid(gate) * emb


def _ffn_ln_ple(alpha, h, ffn, g, b, w_gate, p, w_ple, tm=512):
    t, dm = h.shape
    pd = p.shape[1]
    row = _row_spec(tm, dm)
    vec = _full_spec((1, dm))
    return pl.pallas_call(
        functools.partial(_ffn_ln_ple_kernel, alpha),
        grid=(t // tm,),
        in_specs=[row, row, vec, vec, _full_spec((dm, dm)), _row_spec(tm, pd), _full_spec((pd, dm))],
        out_specs=row,
        out_shape=jax.ShapeDtypeStruct((t, dm), F32),
        compiler_params=_params("parallel"),
        name="ffn_ln_ple",
    )(h, ffn, g.reshape(1, dm), b.reshape(1, dm), w_gate, p, w_ple)


def _s5_tables(lam_re, lam_im, log_dt, b_re, b_im, c_re, c_im):
    ell = SSM_CHUNK
    g, p = lam_re.shape
    dt = jnp.exp(log_dt)[:, None]

    def apow(k):
        mag = jnp.exp(k * lam_re * dt)
        ang = k * lam_im * dt
        return mag * jnp.cos(ang), mag * jnp.sin(ang)

    a_re, a_im = apow(1.0)
    den = lam_re * lam_re + lam_im * lam_im
    f_re = ((a_re - 1.0) * lam_re + a_im * lam_im) / den
    f_im = (a_im * lam_re - (a_re - 1.0) * lam_im) / den
    bb_re = f_re[..., None] * b_re - f_im[..., None] * b_im
    bb_im = f_re[..., None] * b_im + f_im[..., None] * b_re

    ks = jnp.arange(ell + 1, dtype=F32)[:, None, None]
    pw_re, pw_im = apow(ks)

    ca_re = c_re[None] * pw_re[:, :, None, :] - c_im[None] * pw_im[:, :, None, :]
    ca_im = c_re[None] * pw_im[:, :, None, :] + c_im[None] * pw_re[:, :, None, :]
    kk = (jnp.einsum('kgcp,gpd->kgcd', ca_re[:ell], bb_re)
          - jnp.einsum('kgcp,gpd->kgcd', ca_im[:ell], bb_im))

    rp_re, rp_im = pw_re[:ell][::-1], pw_im[:ell][::-1]
    ps_re = rp_re[..., None] * bb_re[None] - rp_im[..., None] * bb_im[None]
    ps_im = rp_re[..., None] * bb_im[None] + rp_im[..., None] * bb_re[None]
    pm = jnp.stack([ps_re, ps_im], axis=2)
    pm = pm.transpose(1, 0, 4, 2, 3).reshape(g, ell * SSM_GROUP, 2, p)

    q = jnp.stack([ca_re[1:], -ca_im[1:]], axis=3)
    q = q.transpose(1, 3, 4, 0, 2).reshape(g, 2, p, ell * SSM_GROUP)
    return kk, pm, q


def _s5_block_tables(lam_re, lam_im, log_dt, b_re, b_im, c_re, c_im, n_steps):
    ell, gc, p = SSM_CHUNK, SSM_GROUP, SSM_STATE
    g = lam_re.shape[0]
    gb = LANES // gc
    nb = g // gb
    kk, pm, q = _s5_tables(lam_re, lam_im, log_dt, b_re, b_im, c_re, c_im)
    same = jnp.eye(gb, dtype=bool)
    zero = jnp.zeros((), BF16)
    k6 = kk.astype(BF16).reshape(ell, nb, gb, gc, gc).transpose(1, 0, 2, 4, 3)
    k_blk = jnp.where(same[None, None, :, None, :, None], k6[:, :, :, :, None, :], zero)
    k_blk = k_blk.reshape(nb, ell, LANES, LANES)
    kz = jnp.concatenate([jnp.zeros((nb, 1, LANES, LANES), BF16), k_blk], axis=1)
    half = ell // 2
    top = jnp.concatenate([kz[:, 1::2], kz[:, 2::2]], axis=-1)
    bottom = jnp.concatenate([kz[:, 0::2][:, :half], kz[:, 1::2]], axis=-1)
    w_tab = jnp.concatenate([top, bottom], axis=-2)
    p6 = pm.astype(BF16).reshape(nb, gb, ell, gc, 2, p).transpose(0, 2, 1, 3, 4, 5)
    p_big = jnp.where(same[None, None, :, None, None, :, None], p6[:, :, :, :, :, None, :], zero)
    p_big = p_big.reshape(nb, ell * LANES, 2 * gb * p)
    q6 = q.astype(BF16).reshape(nb, gb, 2, p, ell, gc).transpose(0, 2, 1, 3, 4, 5)
    q_big = jnp.where(same[None, None, :, None, None, :, None], q6[:, :, :, :, :, None, :], zero)
    q_big = q_big.reshape(nb, 2 * gb * p, ell * LANES)
    dt = jnp.exp(log_dt)[:, None]
    ks = (ell * (2.0 ** jnp.arange(n_steps, dtype=F32)))[None, :, None]
    mag = jnp.exp(ks * (lam_re * dt)[:, None, :])
    ang = ks * (lam_im * dt)[:, None, :]
    are = (mag * jnp.cos(ang)).reshape(nb, gb, n_steps, p).transpose(0, 2, 1, 3).reshape(nb, n_steps, gb * p)
    aim = (mag * jnp.sin(ang)).reshape(nb, gb, n_steps, p).transpose(0, 2, 1, 3).reshape(nb, n_steps, gb * p)
    are2 = jnp.concatenate([are, are], axis=-1)
    aim2 = jnp.concatenate([-aim, aim], axis=-1)
    return w_tab, p_big, q_big, are2, aim2


def _s5_kernel(n_chunks, n_steps, u_ref, d_ref, w_ref, p_ref, q_ref, are_ref, aim_ref, z_ref):
    ell = SSM_CHUNK
    half = ell // 2
    xs = [jnp.concatenate([u_ref[j * n_chunks:(j + 1) * n_chunks, :].astype(BF16)
                           for j in (2 * jp, 2 * jp + 1)], axis=1) for jp in range(half)]
    xcat = jnp.concatenate(xs, axis=1)
    x = jnp.dot(xcat, p_ref[0], preferred_element_type=F32)
    width = x.shape[1]
    n_idx = lax.broadcasted_iota(jnp.int32, x.shape, 0)
    for s in range(n_steps):
        sh = 1 << s
        prev = jnp.where(n_idx >= sh, pltpu.roll(x, sh, 0), 0.0)
        swapped = pltpu.roll(prev, width // 2, 1)
        x = x + are_ref[0, s:s + 1, :] * prev + aim_ref[0, s:s + 1, :] * swapped
    x_start = jnp.where(n_idx >= 1, pltpu.roll(x, 1, 0), 0.0)
    y_state = jnp.dot(x_start.astype(BF16), q_ref[0], preferred_element_type=F32)
    for tp in range(half):
        y = y_state[:, 2 * tp * LANES:2 * (tp + 1) * LANES]
        for jp in range(tp + 1):
            y = y + jnp.dot(xs[jp], w_ref[0, tp - jp], preferred_element_type=F32)
        for tl in range(2):
            rows = slice((2 * tp + tl) * n_chunks, (2 * tp + tl + 1) * n_chunks)
            zj = jax.nn.gelu(y[:, tl * LANES:(tl + 1) * LANES] + d_ref[...] * u_ref[rows, :])
            z_ref[rows, :] = zj.astype(z_ref.dtype)


def _s5_gelu(u, d, bsz, lam_re, lam_im, log_dt, b_re, b_im, c_re, c_im):
    t, dm = u.shape
    seq = t // bsz
    n_chunks = seq // SSM_CHUNK
    assert n_chunks & (n_chunks - 1) == 0, "chunks per sequence must be a power of two"
    n_steps = max(1, (n_chunks - 1).bit_length())
    w_tab, p_big, q_big, are2, aim2 = _s5_block_tables(
        lam_re, lam_im, log_dt, b_re, b_im, c_re, c_im, n_steps)
    nb = dm // LANES
    kw = SSM_CHUNK * LANES
    sw = p_big.shape[2]
    half = SSM_CHUNK // 2
    return pl.pallas_call(
        functools.partial(_s5_kernel, n_chunks, n_steps),
        grid=(nb, bsz),
        in_specs=[
            pl.BlockSpec((seq, LANES), lambda i, b: (b, i)),
            pl.BlockSpec((1, LANES), lambda i, b: (0, i)),
            pl.BlockSpec((1, half, 2 * LANES, 2 * LANES), lambda i, b: (i, 0, 0, 0)),
            pl.BlockSpec((1, kw, sw), lambda i, b: (i, 0, 0)),
            pl.BlockSpec((1, sw, kw), lambda i, b: (i, 0, 0)),
            pl.BlockSpec((1, n_steps, sw), lambda i, b: (i, 0, 0)),
            pl.BlockSpec((1, n_steps, sw), lambda i, b: (i, 0, 0)),
        ],
        out_specs=pl.BlockSpec((seq, LANES), lambda i, b: (b, i)),
        out_shape=jax.ShapeDtypeStruct((t, dm), BF16),
        compiler_params=_params("parallel", "arbitrary"),
        name="s5_gelu",
    )(u, d.reshape(1, dm), w_tab, p_big, q_big, are2, aim2)


def _to_chunk_order(a, bsz):
    t, f = a.shape
    n = t // (bsz * SSM_CHUNK)
    return a.reshape(bsz, n, SSM_CHUNK, f).transpose(0, 2, 1, 3).reshape(t, f)


def _from_chunk_order(a, bsz):
    t, f = a.shape
    n = t // (bsz * SSM_CHUNK)
    return a.reshape(bsz, SSM_CHUNK, n, f).transpose(0, 2, 1, 3).reshape(t, f)


def _sb_scores(qh, kb, tri, diag_mask):
    z2 = lax.dot_general(qh, kb, _NT, preferred_element_type=F32)
    pos = jnp.maximum(z2, 0.0)
    neg = z2 - pos
    soft = jnp.log2(1.0 + jnp.exp2(neg - pos))
    neg_log_keep = pos + soft
    log_beta = neg - soft
    if diag_mask is not None:
        neg_log_keep = jnp.where(diag_mask, neg_log_keep, 0.0)
    later = jnp.dot(neg_log_keep.astype(BF16), tri, preferred_element_type=F32)
    arg = log_beta - later
    if diag_mask is not None:
        arg = jnp.where(diag_mask, arg, -jnp.inf)
    return arg, jnp.sum(neg_log_keep, axis=1, keepdims=True)


def _sb_kernel(bq, scale, q_ref, k_ref, v_ref, o_ref, arg_scr, acc_scr):
    qi = pl.program_id(2)
    n_heads = q_ref.shape[2] // HEAD_DIM
    row = lax.broadcasted_iota(jnp.int32, (bq, bq), 0)
    col = lax.broadcasted_iota(jnp.int32, (bq, bq), 1)
    tri = (row > col).astype(BF16)
    diag_mask = col < row
    lanes = [slice(hh * HEAD_DIM, (hh + 1) * HEAD_DIM) for hh in range(n_heads)]
    qs = [(q_ref[0, :, ls].astype(F32) * scale).astype(BF16) for ls in lanes]

    def rows_of(ref, kblk, ls):
        return ref[0, pl.ds(pl.multiple_of(kblk * bq, bq), bq), ls]

    def finish(hh, carry, vblk):
        w = jnp.exp2(arg_scr[hh] - carry)
        return jnp.dot(w.astype(BF16), rows_of(v_ref, vblk, lanes[hh]), preferred_element_type=F32)

    acc_scr[...] = jnp.zeros_like(acc_scr)
    state = []
    for hh in range(n_heads):
        arg, rowsum = _sb_scores(qs[hh], rows_of(k_ref, qi, lanes[hh]), tri, diag_mask)
        arg_scr[hh] = arg
        state.extend((jnp.zeros((bq, 1), F32), rowsum))

    def body(i, state):
        new = []
        for hh in range(n_heads):
            carry, rowsum = state[2 * hh], state[2 * hh + 1]
            acc_scr[:, lanes[hh]] += finish(hh, carry, qi - i)
            arg, next_rowsum = _sb_scores(qs[hh], rows_of(k_ref, qi - 1 - i, lanes[hh]), tri, None)
            arg_scr[hh] = arg
            new.extend((carry + rowsum, next_rowsum))
        return tuple(new)

    state = lax.fori_loop(0, qi, body, tuple(state))
    outs = [acc_scr[:, lanes[hh]] + finish(hh, state[2 * hh], 0) for hh in range(n_heads)]
    o_ref[0] = jnp.concatenate(outs, axis=1).astype(o_ref.dtype)


def _stick_breaking(q, kv, bsz, bq=256, heads_per_step=8):
    t, dm = q.shape
    seq = t // bsz
    hw = heads_per_step * HEAD_DIM
    n_hb = dm // hw
    q3 = q.reshape(bsz, seq, dm)
    kv3 = kv.reshape(bsz, seq, 2 * dm)
    out = pl.pallas_call(
        functools.partial(_sb_kernel, bq, HEAD_DIM ** -0.5 * math.log2(math.e)),
        grid=(bsz, n_hb, seq // bq),
        in_specs=[
            pl.BlockSpec((1, bq, hw), lambda b, h, i: (b, i, h)),
            pl.BlockSpec((1, seq, hw), lambda b, h, i: (b, 0, h)),
            pl.BlockSpec((1, seq, hw), lambda b, h, i: (b, 0, n_hb + h)),
        ],
        out_specs=pl.BlockSpec((1, bq, hw), lambda b, h, i: (b, i, h)),
        out_shape=jax.ShapeDtypeStruct((bsz, seq, dm), BF16),
        scratch_shapes=[pltpu.VMEM((heads_per_step, bq, bq), F32), pltpu.VMEM((bq, hw), F32)],
        compiler_params=_params("parallel", "parallel", "arbitrary"),
        name="stick_breaking",
    )(q3, kv3, kv3)
    return out.reshape(t, dm)


def _cmpx(a, i, j):
    hi = jnp.maximum(a[i], a[j])
    lo = jnp.minimum(a[i], a[j])
    a[i], a[j] = hi, lo


def _bitonic_merge_desc(a):
    n = len(a)
    j = n // 2
    while j >= 1:
        for i in range(n):
            l = i ^ j
            if l > i:
                _cmpx(a, i, l)
        j //= 2


def _sorted_top16(s):
    k = PEER_TOPK
    a = [s[SUBLANES * i:SUBLANES * (i + 1), :] for i in range(k)]
    size = 2
    while size <= k:
        j = size // 2
        while j >= 1:
            for i in range(k):
                l = i ^ j
                if l > i:
                    if (i & size) == 0:
                        _cmpx(a, i, l)
                    else:
                        _cmpx(a, l, i)
            j //= 2
        size *= 2
    for shift in (4, 2, 1):
        b = [pltpu.roll(x, shift, 0) for x in a]
        a = [jnp.maximum(a[i], b[k - 1 - i]) for i in range(k)]
        _bitonic_merge_desc(a)
    return a


def _peer_select(v1, v2):
    k = PEER_TOPK
    cells = [v1[a] + v2[b] for a in range(k) for b in range(k) if (a + 1) * (b + 1) <= k]
    top = cells[0]
    cur = list(cells)
    remaining = jnp.full(top.shape, float(k), F32)
    tau = top
    neg = jnp.float32(-jnp.inf)
    for _ in range(k):
        m = functools.reduce(jnp.maximum, cur)
        eq = [c == m for c in cur]
        cnt = functools.reduce(jnp.add, [e.astype(F32) for e in eq])
        tau = jnp.where(remaining > 0.0, m, tau)
        remaining = remaining - cnt
        cur = [jnp.where(e, neg, c) for e, c in zip(eq, cur)]
    zsum = functools.reduce(
        jnp.add, [jnp.where(c >= tau, jnp.exp(c - top), 0.0) for c in cells])
    return tau, zsum


def _prefix_count(vals, test):
    m8 = test(vals[7])
    m4 = test(jnp.where(m8, vals[11], vals[3]))
    m2 = test(jnp.where(m8, jnp.where(m4, vals[13], vals[9]), jnp.where(m4, vals[5], vals[1])))
    hi = jnp.where(m4, jnp.where(m2, vals[14], vals[12]), jnp.where(m2, vals[10], vals[8]))
    lo = jnp.where(m4, jnp.where(m2, vals[6], vals[4]), jnp.where(m2, vals[2], vals[0]))
    m1 = test(jnp.where(m8, hi, lo))
    m16 = test(vals[15])
    count = jnp.where(m8, 8.0, 0.0) + jnp.where(m4, 4.0, 0.0) + jnp.where(m2, 2.0, 0.0)
    return count + jnp.where(m1, 1.0, 0.0) + jnp.where(m16, 1.0, 0.0)


def _split_bf16(x):
    hi = x.astype(BF16)
    lo = (x - hi.astype(F32)).astype(BF16)
    return hi, lo


def _scores_t(k_hi, k_lo, qh):
    q_hi, q_lo = _split_bf16(qh)
    s = lax.dot_general(k_hi, q_hi, _NT, preferred_element_type=F32)
    s = s + lax.dot_general(k_hi, q_lo, _NT, preferred_element_type=F32)
    return s + lax.dot_general(k_lo, q_hi, _NT, preferred_element_type=F32)


def _peer_kernel(n_e1, n_chunks, x_ref, wq_ref, k1_ref, k2_ref, u_ref, vt_ref, o_ref,
                 xt_scr, k1_scr, k2_scr, ga_a, ga_b, w_a, w_b, acc_scr):
    c = pl.program_id(1)
    tn = x_ref.shape[0]
    nk = PEER_NKEYS
    n_lt = tn // LANES

    @pl.when(c == 0)
    def _():
        xb = x_ref[...].astype(BF16)
        xt_scr[...] = x_ref[...].T.astype(BF16)
        q = jnp.dot(xb, wq_ref[...], preferred_element_type=F32)
        k1_hi, k1_lo = _split_bf16(k1_ref[...])
        k2_hi, k2_lo = _split_bf16(k2_ref[...])
        sub = lax.broadcasted_iota(jnp.int32, (SUBLANES, tn), 0)
        v1s = [jnp.zeros((SUBLANES, tn), F32) for _ in range(PEER_TOPK)]
        v2s = [jnp.zeros((SUBLANES, tn), F32) for _ in range(PEER_TOPK)]
        s1_all, s2_all = [], []
        for h in range(PEER_HEADS):
            s1 = _scores_t(k1_hi, k1_lo, q[:, 2 * nk * h:2 * nk * h + nk])
            s2 = _scores_t(k2_hi, k2_lo, q[:, 2 * nk * h + nk:2 * nk * (h + 1)])
            s1_all.append(s1)
            s2_all.append(s2)
            t1 = _sorted_top16(s1)
            t2 = _sorted_top16(s2)
            for a in range(PEER_TOPK):
                v1s[a] = jnp.where(sub == h, t1[a], v1s[a])
                v2s[a] = jnp.where(sub == h, t2[a], v2s[a])
        tau, zsum = _peer_select(v1s, v2s)
        inv_z = 1.0 / zsum
        for h in range(PEER_HEADS):
            s1, s2 = s1_all[h], s2_all[h]
            hr = slice(h, h + 1)
            v2_rows = [v2s[b][hr, :] for b in range(PEER_TOPK)]
            tau_h = tau[hr, :]
            c1 = _prefix_count(v2_rows, lambda row: s1 + row >= tau_h)
            r2 = _prefix_count(v2_rows, lambda row: row > s2)
            p1 = jnp.exp(s1 - v1s[0][hr, :]) * inv_z[hr, :]
            r2 = r2.astype(BF16)
            p2 = jnp.exp(s2 - v2s[0][hr, :]).astype(BF16)
            for lt in range(n_lt):
                ls = slice(lt * LANES, (lt + 1) * LANES)
                for kb in range(nk // SUBLANES):
                    rs = slice(kb * SUBLANES, (kb + 1) * SUBLANES)
                    k1_scr[kb, lt, h, 0] = c1[rs, ls]
                    k1_scr[kb, lt, h, 1] = p1[rs, ls]
                for jb in range(nk // PACKED_ROWS):
                    rs = slice(jb * PACKED_ROWS, (jb + 1) * PACKED_ROWS)
                    k2_scr[jb, lt, h, 0] = r2[rs, ls]
                    k2_scr[jb, lt, h, 1] = p2[rs, ls]
        acc_scr[...] = jnp.zeros_like(acc_scr)
        for ref in (ga_a, ga_b, w_a, w_b):
            ref[...] = jnp.zeros_like(ref)

    chunk1 = jnp.clip(c - 1, 0, n_chunks - 1)
    zero = jnp.zeros((PACKED_ROWS, LANES), BF16)

    def gate_rows(e, ga_old, w_new):
        blk = chunk1 * (n_e1 // SUBLANES) + e // SUBLANES
        r = e % SUBLANES
        for lt in range(n_lt):
            ls = slice(lt * LANES, (lt + 1) * LANES)
            c1b, p1b = [], []
            for h in range(PEER_HEADS):
                c1_row = k1_scr[blk, lt, h, 0][r:r + 1, :]
                p1_row = k1_scr[blk, lt, h, 1][r:r + 1, :]
                c1b.append(jnp.broadcast_to(c1_row, (PACKED_ROWS, LANES)).astype(BF16))
                p1b.append(jnp.broadcast_to(p1_row, (PACKED_ROWS, LANES)).astype(BF16))
            for jb in range(nk // PACKED_ROWS):
                g = zero
                for h in range(PEER_HEADS):
                    g = g + p1b[h] * jnp.where(k2_scr[jb, lt, h, 0] < c1b[h], k2_scr[jb, lt, h, 1], zero)
                ws = slice(e * nk + jb * PACKED_ROWS, e * nk + (jb + 1) * PACKED_ROWS)
                w_new[ws, ls] = g * ga_old[ws, ls]

    def tick(ga_new, ga_old, w_new, w_old):
        e_sub = 2
        part = None
        for s in range(n_e1 // e_sub):
            rows = slice(s * e_sub * nk, (s + 1) * e_sub * nk)
            d = jnp.dot(vt_ref[:, rows], w_old[rows, :], preferred_element_type=F32)
            part = d if part is None else part + d
            for eo in range(e_sub):
                gate_rows(s * e_sub + eo, ga_old, w_new)
            act = jnp.dot(u_ref[rows, :].astype(BF16), xt_scr[...], preferred_element_type=F32)
            ga_new[rows, :] = jax.nn.gelu(act.astype(BF16))
        acc_scr[...] += part

    @pl.when(c % 2 == 0)
    def _():
        tick(ga_a, ga_b, w_b, w_a)

    @pl.when(c % 2 == 1)
    def _():
        tick(ga_b, ga_a, w_a, w_b)

    @pl.when(c == pl.num_programs(1) - 1)
    def _():
        o_ref[...] = acc_scr[...].T


def _peer(x, wq, k1, k2, u_all, v_all, layer, tn=512, n_e1=8):
    assert n_e1 % SUBLANES == 0
    t, dm = x.shape
    n_exp = u_all.shape[1]
    ce = n_e1 * PEER_NKEYS
    n_chunks = n_exp // ce
    hq = wq.shape[1]
    vt_tab = v_all[layer].reshape(n_chunks, ce, dm).transpose(0, 2, 1).astype(BF16)
    n_lt = tn // LANES
    k1_tab = pltpu.VMEM((PEER_NKEYS // SUBLANES, n_lt, PEER_HEADS, 2, SUBLANES, LANES), F32)
    k2_tab = pltpu.VMEM((PEER_NKEYS // PACKED_ROWS, n_lt, PEER_HEADS, 2, PACKED_ROWS, LANES), BF16)
    last = n_chunks - 1
    return pl.pallas_call(
        functools.partial(_peer_kernel, n_e1, n_chunks),
        grid=(t // tn, n_chunks + 2),
        in_specs=[
            pl.BlockSpec((tn, dm), lambda i, c: (i, 0)),
            pl.BlockSpec((dm, hq), lambda i, c: (0, 0)),
            pl.BlockSpec((PEER_NKEYS, PEER_NKEYS), lambda i, c: (0, 0)),
            pl.BlockSpec((PEER_NKEYS, PEER_NKEYS), lambda i, c: (0, 0)),
            pl.BlockSpec((None, ce, dm), lambda i, c: (layer, jnp.minimum(c, last), 0)),
            pl.BlockSpec((None, dm, ce), lambda i, c: (jnp.clip(c - 2, 0, last), 0, 0)),
        ],
        out_specs=pl.BlockSpec((tn, dm), lambda i, c: (i, 0)),
        out_shape=jax.ShapeDtypeStruct((t, dm), F32),
        scratch_shapes=[
            pltpu.VMEM((dm, tn), BF16),
            k1_tab, k2_tab,
            pltpu.VMEM((ce, tn), BF16), pltpu.VMEM((ce, tn), BF16),
            pltpu.VMEM((ce, tn), BF16), pltpu.VMEM((ce, tn), BF16),
            pltpu.VMEM((dm, tn), F32),
        ],
        compiler_params=_params("parallel", "arbitrary"),
        name="peer",
    )(x, wq, k1, k2, u_all, vt_tab)


@jax.jit
def _trunk(x, p, ln_mix_g, ln_mix_b, ln_ffn_g, ln_ffn_b, ssm_w_in, ssm_lam_re, ssm_lam_im,
           ssm_log_dt, ssm_b_re, ssm_b_im, ssm_c_re, ssm_c_im, ssm_d, ssm_w_glu,
           kv_ln_g, kv_ln_b, w_kv, sb_w_q, sb_w_o, peer_w_q, peer_k1, peer_k2,
           peer_u, peer_v, ple_w, ple_gate):
    bsz, seq, dm = x.shape
    depth = p.shape[0]
    n_a = ssm_w_in.shape[0]
    alpha = (2 * depth) ** 0.25
    t = bsz * seq
    h = x.reshape(t, dm)
    if n_a > 0:
        h = _to_chunk_order(h, bsz)
    kv = None
    for i in range(depth):
        p_i = p[i].reshape(t, -1)
        if i < n_a:
            p_i = _to_chunk_order(p_i, bsz)
            u = _mm(h, ssm_w_in[i].astype(BF16))
            z = _s5_gelu(u, ssm_d[i], bsz, ssm_lam_re[i], ssm_lam_im[i], ssm_log_dt[i],
                         ssm_b_re[i], ssm_b_im[i], ssm_c_re[i], ssm_c_im[i])
            wglu = ssm_w_glu[i].astype(BF16)
            h = _glu_ln(alpha, z, wglu[:, :dm], wglu[:, dm:], h, ln_mix_g[i], ln_mix_b[i])
        else:
            if i == n_a:
                kv = _ln_mm(h, kv_ln_g, kv_ln_b, w_kv.astype(BF16), out_dtype=BF16)
            j = i - n_a
            q = _mm(h, sb_w_q[j].astype(BF16), out_dtype=BF16)
            o = _stick_breaking(q, kv, bsz)
            h = _proj_ln(alpha, o, sb_w_o[j].astype(BF16), h, ln_mix_g[i], ln_mix_b[i])
        ffn = _peer(h, peer_w_q[i].astype(BF16), peer_k1[i], peer_k2[i],
                    peer_u, peer_v, i)
        h = _ffn_ln_ple(alpha, h, ffn, ln_ffn_g[i], ln_ffn_b[i], ple_gate[i].astype(BF16),
                        p_i, ple_w[i].astype(BF16))
        if i == n_a - 1:
            h = _from_chunk_order(h, bsz)
    return h.reshape(bsz, seq, dm)


def kernel(x, p, ln_mix_g, ln_mix_b, ln_ffn_g, ln_ffn_b, ssm_w_in, ssm_lam_re, ssm_lam_im, ssm_log_dt, ssm_b_re, ssm_b_im, ssm_c_re, ssm_c_im, ssm_d, ssm_w_glu, kv_ln_g, kv_ln_b, w_kv, sb_w_q, sb_w_o, peer_w_q, peer_k1, peer_k2, peer_u, peer_v, ple_w, ple_gate):
    return _trunk(x, p, ln_mix_g, ln_mix_b, ln_ffn_g, ln_ffn_b, ssm_w_in, ssm_lam_re, ssm_lam_im,
                  ssm_log_dt, ssm_b_re, ssm_b_im, ssm_c_re, ssm_c_im, ssm_d, ssm_w_glu,
                  kv_ln_g, kv_ln_b, w_kv, sb_w_q, sb_w_o, peer_w_q, peer_k1, peer_k2,
                  peer_u, peer_v, ple_w, ple_gate)
```

```python
import functools
import math

import jax
import jax.numpy as jnp
from jax import lax
from jax.experimental import pallas as pl
from jax.experimental.pallas import tpu as pltpu

F32 = jnp.float32
BF16 = jnp.bfloat16

HEAD_DIM = 64
SSM_GROUP = 16
SSM_STATE = 64
SSM_CHUNK = 16
PEER_HEADS = 8
PEER_NKEYS = 128
PEER_TOPK = 16
LN_EPS = 1e-5

LANES = 128
SUBLANES = 8
PACKED_ROWS = 16
VMEM_LIMIT = 56 * 1024 * 1024

_NT = (((1,), (1,)), ((), ()))


def _params(*sem, flags=None):
    return pltpu.CompilerParams(dimension_semantics=sem, vmem_limit_bytes=VMEM_LIMIT, flags=flags)


def _ln(x, g, b):
    mu = jnp.mean(x, axis=-1, keepdims=True)
    xc = x - mu
    var = jnp.mean(xc * xc, axis=-1, keepdims=True)
    return xc * lax.rsqrt(var + LN_EPS) * g + b


def _sigmoid(x):
    return 1.0 / (1.0 + jnp.exp(-x))


def _row_spec(tm, n):
    return pl.BlockSpec((tm, n), lambda i: (i, 0))


def _full_spec(shape):
    nd = len(shape)
    return pl.BlockSpec(shape, lambda i: (0,) * nd)


def _mm_kernel(x_ref, w_ref, o_ref):
    acc = jnp.dot(x_ref[...].astype(BF16), w_ref[...], preferred_element_type=F32)
    o_ref[...] = acc.astype(o_ref.dtype)


def _mm(x, w, out_dtype=F32, tm=512):
    t, k = x.shape
    n = w.shape[1]
    return pl.pallas_call(
        _mm_kernel,
        grid=(t // tm,),
        in_specs=[_row_spec(tm, k), _full_spec((k, n))],
        out_specs=_row_spec(tm, n),
        out_shape=jax.ShapeDtypeStruct((t, n), out_dtype),
        compiler_params=_params("parallel"),
        name="mm",
    )(x, w)


def _ln_mm_kernel(x_ref, g_ref, b_ref, w_ref, o_ref):
    y = _ln(x_ref[...], g_ref[...], b_ref[...])
    acc = jnp.dot(y.astype(BF16), w_ref[...], preferred_element_type=F32)
    o_ref[...] = acc.astype(o_ref.dtype)


def _ln_mm(x, g, b, w, out_dtype=F32, tm=512):
    t, k = x.shape
    n = w.shape[1]
    return pl.pallas_call(
        _ln_mm_kernel,
        grid=(t // tm,),
        in_specs=[_row_spec(tm, k), _full_spec((1, k)), _full_spec((1, k)), _full_spec((k, n))],
        out_specs=_row_spec(tm, n),
        out_shape=jax.ShapeDtypeStruct((t, n), out_dtype),
        compiler_params=_params("parallel"),
        name="ln_mm",
    )(x, g.reshape(1, k), b.reshape(1, k), w)


def _glu_ln_kernel(alpha, z_ref, wv_ref, wg_ref, h_ref, g_ref, b_ref, o_ref):
    z = z_ref[...]
    val = jnp.dot(z, wv_ref[...], preferred_element_type=F32)
    gate = jnp.dot(z, wg_ref[...], preferred_element_type=F32)
    mix = val * _sigmoid(gate)
    o_ref[...] = _ln(alpha * h_ref[...] + mix, g_ref[...], b_ref[...])


def _glu_ln(alpha, z, wv, wg, h, g, b, tm=512):
    t, dm = h.shape
    row = _row_spec(tm, dm)
    vec = _full_spec((1, dm))
    return pl.pallas_call(
        functools.partial(_glu_ln_kernel, alpha),
        grid=(t // tm,),
        in_specs=[row, _full_spec((dm, dm)), _full_spec((dm, dm)), row, vec, vec],
        out_specs=row,
        out_shape=jax.ShapeDtypeStruct((t, dm), F32),
        compiler_params=_params("parallel"),
        name="glu_ln",
    )(z, wv, wg, h, g.reshape(1, dm), b.reshape(1, dm))


def _proj_ln_kernel(alpha, o_in_ref, w_ref, h_ref, g_ref, b_ref, o_ref):
    mix = jnp.dot(o_in_ref[...], w_ref[...], preferred_element_type=F32)
    o_ref[...] = _ln(alpha * h_ref[...] + mix, g_ref[...], b_ref[...])


def _proj_ln(alpha, o, w, h, g, b, tm=512):
    t, dm = h.shape
    row = _row_spec(tm, dm)
    vec = _full_spec((1, dm))
    return pl.pallas_call(
        functools.partial(_proj_ln_kernel, alpha),
        grid=(t // tm,),
        in_specs=[row, _full_spec((dm, dm)), row, vec, vec],
        out_specs=row,
        out_shape=jax.ShapeDtypeStruct((t, dm), F32),
        compiler_params=_params("parallel"),
        name="proj_ln",
    )(o, w, h, g.reshape(1, dm), b.reshape(1, dm))


def _ffn_ln_ple_kernel(alpha, h_ref, f_ref, g_ref, b_ref, wg_ref, p_ref, wp_ref, o_ref):
    h2 = _ln(alpha * h_ref[...] + f_ref[...], g_ref[...], b_ref[...])
    gate = jnp.dot(h2.astype(BF16), wg_ref[...], preferred_element_type=F32)
    emb = jnp.dot(p_ref[...].astype(BF16), wp_ref[...], preferred_element_type=F32)
    o_ref[...] = h2 + _sigmoid(gate) * emb


def _ffn_ln_ple(alpha, h, ffn, g, b, w_gate, p, w_ple, tm=512):
    t, dm = h.shape
    pd = p.shape[1]
    row = _row_spec(tm, dm)
    vec = _full_spec((1, dm))
    return pl.pallas_call(
        functools.partial(_ffn_ln_ple_kernel, alpha),
        grid=(t // tm,),
        in_specs=[row, row, vec, vec, _full_spec((dm, dm)), _row_spec(tm, pd), _full_spec((pd, dm))],
        out_specs=row,
        out_shape=jax.ShapeDtypeStruct((t, dm), F32),
        compiler_params=_params("parallel"),
        name="ffn_ln_ple",
    )(h, ffn, g.reshape(1, dm), b.reshape(1, dm), w_gate, p, w_ple)


def _s5_tables(lam_re, lam_im, log_dt, b_re, b_im, c_re, c_im):
    ell = SSM_CHUNK
    g, p = lam_re.shape
    dt = jnp.exp(log_dt)[:, None]

    def apow(k):
        mag = jnp.exp(k * lam_re * dt)
        ang = k * lam_im * dt
        return mag * jnp.cos(ang), mag * jnp.sin(ang)

    a_re, a_im = apow(1.0)
    den = lam_re * lam_re + lam_im * lam_im
    f_re = ((a_re - 1.0) * lam_re + a_im * lam_im) / den
    f_im = (a_im * lam_re - (a_re - 1.0) * lam_im) / den
    bb_re = f_re[..., None] * b_re - f_im[..., None] * b_im
    bb_im = f_re[..., None] * b_im + f_im[..., None] * b_re

    ks = jnp.arange(ell + 1, dtype=F32)[:, None, None]
    pw_re, pw_im = apow(ks)

    ca_re = c_re[None] * pw_re[:, :, None, :] - c_im[None] * pw_im[:, :, None, :]
    ca_im = c_re[None] * pw_im[:, :, None, :] + c_im[None] * pw_re[:, :, None, :]
    kk = (jnp.einsum('kgcp,gpd->kgcd', ca_re[:ell], bb_re)
          - jnp.einsum('kgcp,gpd->kgcd', ca_im[:ell], bb_im))

    rp_re, rp_im = pw_re[:ell][::-1], pw_im[:ell][::-1]
    ps_re = rp_re[..., None] * bb_re[None] - rp_im[..., None] * bb_im[None]
    ps_im = rp_re[..., None] * bb_im[None] + rp_im[..., None] * bb_re[None]
    pm = jnp.stack([ps_re, ps_im], axis=2)
    pm = pm.transpose(1, 0, 4, 2, 3).reshape(g, ell * SSM_GROUP, 2, p)

    q = jnp.stack([ca_re[1:], -ca_im[1:]], axis=3)
    q = q.transpose(1, 3, 4, 0, 2).reshape(g, 2, p, ell * SSM_GROUP)
    return kk, pm, q


def _s5_block_tables(lam_re, lam_im, log_dt, b_re, b_im, c_re, c_im, n_steps):
    ell, gc, p = SSM_CHUNK, SSM_GROUP, SSM_STATE
    g = lam_re.shape[0]
    gb = LANES // gc
    nb = g // gb
    kk, pm, q = _s5_tables(lam_re, lam_im, log_dt, b_re, b_im, c_re, c_im)
    same = jnp.eye(gb, dtype=bool)
    zero = jnp.zeros((), BF16)
    k6 = kk.astype(BF16).reshape(ell, nb, gb, gc, gc).transpose(1, 0, 2, 4, 3)
    k_blk = jnp.where(same[None, None, :, None, :, None], k6[:, :, :, :, None, :], zero)
    k_blk = k_blk.reshape(nb, ell, LANES, LANES)
    kz = jnp.concatenate([jnp.zeros((nb, 1, LANES, LANES), BF16), k_blk], axis=1)
    half = ell // 2
    top = jnp.concatenate([kz[:, 1::2], kz[:, 2::2]], axis=-1)
    bottom = jnp.concatenate([kz[:, 0::2][:, :half], kz[:, 1::2]], axis=-1)
    w_tab = jnp.concatenate([top, bottom], axis=-2)
    p6 = pm.astype(BF16).reshape(nb, gb, ell, gc, 2, p).transpose(0, 2, 1, 3, 4, 5)
    p_big = jnp.where(same[None, None, :, None, None, :, None], p6[:, :, :, :, :, None, :], zero)
    p_big = p_big.reshape(nb, ell * LANES, 2 * gb * p)
    q6 = q.astype(BF16).reshape(nb, gb, 2, p, ell, gc).transpose(0, 2, 1, 3, 4, 5)
    q_big = jnp.where(same[None, None, :, None, None, :, None], q6[:, :, :, :, :, None, :], zero)
    q_big = q_big.reshape(nb, 2 * gb * p, ell * LANES)
    dt = jnp.exp(log_dt)[:, None]
    ks = (ell * (2.0 ** jnp.arange(n_steps, dtype=F32)))[None, :, None]
    mag = jnp.exp(ks * (lam_re * dt)[:, None, :])
    ang = ks * (lam_im * dt)[:, None, :]
    are = (mag * jnp.cos(ang)).reshape(nb, gb, n_steps, p).transpose(0, 2, 1, 3).reshape(nb, n_steps, gb * p)
    aim = (mag * jnp.sin(ang)).reshape(nb, gb, n_steps, p).transpose(0, 2, 1, 3).reshape(nb, n_steps, gb * p)
    are2 = jnp.concatenate([are, are], axis=-1)
    aim2 = jnp.concatenate([-aim, aim], axis=-1)
    return w_tab, p_big, q_big, are2, aim2


def _s5_kernel(n_chunks, n_steps, u_ref, d_ref, w_ref, p_ref, q_ref, are_ref, aim_ref, z_ref):
    ell = SSM_CHUNK
    half = ell // 2
    xs = [jnp.concatenate([u_ref[j * n_chunks:(j + 1) * n_chunks, :].astype(BF16)
                           for j in (2 * jp, 2 * jp + 1)], axis=1) for jp in range(half)]
    xcat = jnp.concatenate(xs, axis=1)
    x = jnp.dot(xcat, p_ref[0], preferred_element_type=F32)
    width = x.shape[1]
    n_idx = lax.broadcasted_iota(jnp.int32, x.shape, 0)
    for s in range(n_steps):
        sh = 1 << s
        prev = jnp.where(n_idx >= sh, pltpu.roll(x, sh, 0), 0.0)
        swapped = pltpu.roll(prev, width // 2, 1)
        x = x + are_ref[0, s:s + 1, :] * prev + aim_ref[0, s:s + 1, :] * swapped
    x_start = jnp.where(n_idx >= 1, pltpu.roll(x, 1, 0), 0.0)
    y_state = jnp.dot(x_start.astype(BF16), q_ref[0], preferred_element_type=F32)
    for tp in range(half):
        y = y_state[:, 2 * tp * LANES:2 * (tp + 1) * LANES]
        for jp in range(tp + 1):
            y = y + jnp.dot(xs[jp], w_ref[0, tp - jp], preferred_element_type=F32)
        for tl in range(2):
            rows = slice((2 * tp + tl) * n_chunks, (2 * tp + tl + 1) * n_chunks)
            zj = jax.nn.gelu(y[:, tl * LANES:(tl + 1) * LANES] + d_ref[...] * u_ref[rows, :])
            z_ref[rows, :] = zj.astype(z_ref.dtype)


def _s5_gelu(u, d, bsz, lam_re, lam_im, log_dt, b_re, b_im, c_re, c_im):
    t, dm = u.shape
    seq = t // bsz
    n_chunks = seq // SSM_CHUNK
    assert n_chunks & (n_chunks - 1) == 0, "chunks per sequence must be a power of two"
    n_steps = max(1, (n_chunks - 1).bit_length())
    w_tab, p_big, q_big, are2, aim2 = _s5_block_tables(
        lam_re, lam_im, log_dt, b_re, b_im, c_re, c_im, n_steps)
    nb = dm // LANES
    kw = SSM_CHUNK * LANES
    sw = p_big.shape[2]
    half = SSM_CHUNK // 2
    return pl.pallas_call(
        functools.partial(_s5_kernel, n_chunks, n_steps),
        grid=(nb, bsz),
        in_specs=[
            pl.BlockSpec((seq, LANES), lambda i, b: (b, i)),
            pl.BlockSpec((1, LANES), lambda i, b: (0, i)),
            pl.BlockSpec((1, half, 2 * LANES, 2 * LANES), lambda i, b: (i, 0, 0, 0)),
            pl.BlockSpec((1, kw, sw), lambda i, b: (i, 0, 0)),
            pl.BlockSpec((1, sw, kw), lambda i, b: (i, 0, 0)),
            pl.BlockSpec((1, n_steps, sw), lambda i, b: (i, 0, 0)),
            pl.BlockSpec((1, n_steps, sw), lambda i, b: (i, 0, 0)),
        ],
        out_specs=pl.BlockSpec((seq, LANES), lambda i, b: (b, i)),
        out_shape=jax.ShapeDtypeStruct((t, dm), BF16),
        compiler_params=_params("parallel", "arbitrary"),
        name="s5_gelu",
    )(u, d.reshape(1, dm), w_tab, p_big, q_big, are2, aim2)


def _to_chunk_order(a, bsz):
    t, f = a.shape
    n = t // (bsz * SSM_CHUNK)
    return a.reshape(bsz, n, SSM_CHUNK, f).transpose(0, 2, 1, 3).reshape(t, f)


def _from_chunk_order(a, bsz):
    t, f = a.shape
    n = t // (bsz * SSM_CHUNK)
    return a.reshape(bsz, SSM_CHUNK, n, f).transpose(0, 2, 1, 3).reshape(t, f)


def _sb_pointwise(z2):
    pos = jnp.maximum(z2, 0.0)
    neg = z2 - pos
    soft = jnp.log2(1.0 + jnp.exp2(neg - pos))
    return pos + soft, neg - soft


def _sb_scores(qh, kb, tri, diag_mask):
    z2 = lax.dot_general(qh, kb, _NT, preferred_element_type=F32)
    neg_log_keep, log_beta = _sb_pointwise(z2)
    if diag_mask is not None:
        neg_log_keep = jnp.where(diag_mask, neg_log_keep, 0.0)
    later = jnp.dot(neg_log_keep.astype(BF16), tri, preferred_element_type=F32)
    arg = log_beta - later
    if diag_mask is not None:
        arg = jnp.where(diag_mask, arg, -jnp.inf)
    return arg, jnp.sum(neg_log_keep, axis=1, keepdims=True)


def _sb_kernel(bq, scale, q_ref, k_ref, v_ref, o_ref, arg_scr, acc_scr):
    qi = pl.program_id(2)
    n_heads = q_ref.shape[2] // HEAD_DIM
    row = lax.broadcasted_iota(jnp.int32, (bq, bq), 0)
    col = lax.broadcasted_iota(jnp.int32, (bq, bq), 1)
    tri = (row > col).astype(BF16)
    diag_mask = col < row
    lanes = [slice(hh * HEAD_DIM, (hh + 1) * HEAD_DIM) for hh in range(n_heads)]
    qs = [(q_ref[0, :, ls].astype(F32) * scale).astype(BF16) for ls in lanes]

    def rows_of(ref, kblk, ls):
        return ref[0, pl.ds(pl.multiple_of(kblk * bq, bq), bq), ls]

    def finish(hh, carry, vblk):
        w = jnp.exp2(arg_scr[hh] - carry)
        return jnp.dot(w.astype(BF16), rows_of(v_ref, vblk, lanes[hh]), preferred_element_type=F32)

    acc_scr[...] = jnp.zeros_like(acc_scr)
    state = []
    for hh in range(n_heads):
        arg, rowsum = _sb_scores(qs[hh], rows_of(k_ref, qi, lanes[hh]), tri, diag_mask)
        arg_scr[hh] = arg
        state.extend((jnp.zeros((bq, 1), F32), rowsum))

    def body(i, state):
        new = [None] * (2 * n_heads)
        group = 4
        for g0 in range(0, n_heads, group):
            heads = range(g0, min(g0 + group, n_heads))
            z2 = {hh: lax.dot_general(qs[hh], rows_of(k_ref, qi - 1 - i, lanes[hh]), _NT,
                                      preferred_element_type=F32) for hh in heads}
            w = {hh: jnp.exp2(arg_scr[hh] - state[2 * hh]).astype(BF16) for hh in heads}
            point = {hh: _sb_pointwise(z2[hh]) for hh in heads}
            for hh in heads:
                acc_scr[:, lanes[hh]] += jnp.dot(w[hh], rows_of(v_ref, qi - i, lanes[hh]),
                                                 preferred_element_type=F32)
            later = {hh: jnp.dot(point[hh][0].astype(BF16), tri, preferred_element_type=F32)
                     for hh in heads}
            for hh in heads:
                arg_scr[hh] = point[hh][1] - later[hh]
                new[2 * hh] = state[2 * hh] + state[2 * hh + 1]
                new[2 * hh + 1] = jnp.sum(point[hh][0], axis=1, keepdims=True)
        return tuple(new)

    state = lax.fori_loop(0, qi, body, tuple(state))
    outs = [acc_scr[:, lanes[hh]] + finish(hh, state[2 * hh], 0) for hh in range(n_heads)]
    o_ref[0] = jnp.concatenate(outs, axis=1).astype(o_ref.dtype)


def _stick_breaking(q, kv, bsz, bq=256, heads_per_step=8):
    t, dm = q.shape
    seq = t // bsz
    hw = heads_per_step * HEAD_DIM
    n_hb = dm // hw
    q3 = q.reshape(bsz, seq, dm)
    kv3 = kv.reshape(bsz, seq, 2 * dm)
    out = pl.pallas_call(
        functools.partial(_sb_kernel, bq, HEAD_DIM ** -0.5 * math.log2(math.e)),
        grid=(bsz, n_hb, seq // bq),
        in_specs=[
            pl.BlockSpec((1, bq, hw), lambda b, h, i: (b, i, h)),
            pl.BlockSpec((1, seq, hw), lambda b, h, i: (b, 0, h)),
            pl.BlockSpec((1, seq, hw), lambda b, h, i: (b, 0, n_hb + h)),
        ],
        out_specs=pl.BlockSpec((1, bq, hw), lambda b, h, i: (b, i, h)),
        out_shape=jax.ShapeDtypeStruct((bsz, seq, dm), BF16),
        scratch_shapes=[pltpu.VMEM((heads_per_step, bq, bq), F32), pltpu.VMEM((bq, hw), F32)],
        compiler_params=_params("parallel", "parallel", "arbitrary"),
        name="stick_breaking",
    )(q3, kv3, kv3)
    return out.reshape(t, dm)


def _cmpx(a, i, j):
    hi = jnp.maximum(a[i], a[j])
    lo = jnp.minimum(a[i], a[j])
    a[i], a[j] = hi, lo


def _bitonic_merge_desc(a):
    n = len(a)
    j = n // 2
    while j >= 1:
        for i in range(n):
            l = i ^ j
            if l > i:
                _cmpx(a, i, l)
        j //= 2


def _sorted_top16(s):
    k = PEER_TOPK
    a = [s[SUBLANES * i:SUBLANES * (i + 1), :] for i in range(k)]
    size = 2
    while size <= k:
        j = size // 2
        while j >= 1:
            for i in range(k):
                l = i ^ j
                if l > i:
                    if (i & size) == 0:
                        _cmpx(a, i, l)
                    else:
                        _cmpx(a, l, i)
            j //= 2
        size *= 2
    for shift in (4, 2, 1):
        b = [pltpu.roll(x, shift, 0) for x in a]
        a = [jnp.maximum(a[i], b[k - 1 - i]) for i in range(k)]
        _bitonic_merge_desc(a)
    return a


def _peer_select(v1, v2):
    k = PEER_TOPK
    cells = [v1[a] + v2[b] for a in range(k) for b in range(k) if (a + 1) * (b + 1) <= k]
    top = cells[0]
    cur = list(cells)
    remaining = jnp.full(top.shape, float(k), F32)
    tau = top
    neg = jnp.float32(-jnp.inf)
    for _ in range(k):
        m = functools.reduce(jnp.maximum, cur)
        eq = [c == m for c in cur]
        cnt = functools.reduce(jnp.add, [e.astype(F32) for e in eq])
        tau = jnp.where(remaining > 0.0, m, tau)
        remaining = remaining - cnt
        cur = [jnp.where(e, neg, c) for e, c in zip(eq, cur)]
    zsum = functools.reduce(
        jnp.add, [jnp.where(c >= tau, jnp.exp(c - top), 0.0) for c in cells])
    return tau, zsum


def _prefix_count(vals, test):
    m8 = test(vals[7])
    m4 = test(jnp.where(m8, vals[11], vals[3]))
    m2 = test(jnp.where(m8, jnp.where(m4, vals[13], vals[9]), jnp.where(m4, vals[5], vals[1])))
    hi = jnp.where(m4, jnp.where(m2, vals[14], vals[12]), jnp.where(m2, vals[10], vals[8]))
    lo = jnp.where(m4, jnp.where(m2, vals[6], vals[4]), jnp.where(m2, vals[2], vals[0]))
    m1 = test(jnp.where(m8, hi, lo))
    m16 = test(vals[15])
    count = jnp.where(m8, 8.0, 0.0) + jnp.where(m4, 4.0, 0.0) + jnp.where(m2, 2.0, 0.0)
    return count + jnp.where(m1, 1.0, 0.0) + jnp.where(m16, 1.0, 0.0)


def _split_bf16(x):
    hi = x.astype(BF16)
    lo = (x - hi.astype(F32)).astype(BF16)
    return hi, lo


def _scores_t(k_hi, k_lo, qh):
    q_hi, q_lo = _split_bf16(qh)
    s = lax.dot_general(k_hi, q_hi, _NT, preferred_element_type=F32)
    s = s + lax.dot_general(k_hi, q_lo, _NT, preferred_element_type=F32)
    return s + lax.dot_general(k_lo, q_hi, _NT, preferred_element_type=F32)


def _peer_kernel(n_e1, n_chunks, x_ref, wq_ref, k1_ref, k2_ref, u_ref, vt_ref, o_ref,
                 xt_scr, k1_scr, k2_scr, ga_a, ga_b, w_a, w_b, acc_scr):
    c = pl.program_id(1)
    tn = x_ref.shape[0]
    nk = PEER_NKEYS
    n_lt = tn // LANES

    @pl.when(c == 0)
    def _():
        xb = x_ref[...].astype(BF16)
        xt_scr[...] = x_ref[...].T.astype(BF16)
        q = jnp.dot(xb, wq_ref[...], preferred_element_type=F32)
        k1_hi, k1_lo = _split_bf16(k1_ref[...])
        k2_hi, k2_lo = _split_bf16(k2_ref[...])
        sub = lax.broadcasted_iota(jnp.int32, (SUBLANES, tn), 0)
        v1s = [jnp.zeros((SUBLANES, tn), F32) for _ in range(PEER_TOPK)]
        v2s = [jnp.zeros((SUBLANES, tn), F32) for _ in range(PEER_TOPK)]
        s1_all, s2_all = [], []
        for h in range(PEER_HEADS):
            s1 = _scores_t(k1_hi, k1_lo, q[:, 2 * nk * h:2 * nk * h + nk])
            s2 = _scores_t(k2_hi, k2_lo, q[:, 2 * nk * h + nk:2 * nk * (h + 1)])
            s1_all.append(s1)
            s2_all.append(s2)
            t1 = _sorted_top16(s1)
            t2 = _sorted_top16(s2)
            for a in range(PEER_TOPK):
                v1s[a] = jnp.where(sub == h, t1[a], v1s[a])
                v2s[a] = jnp.where(sub == h, t2[a], v2s[a])
        tau, zsum = _peer_select(v1s, v2s)
        inv_z = 1.0 / zsum
        for h in range(PEER_HEADS):
            s1, s2 = s1_all[h], s2_all[h]
            hr = slice(h, h + 1)
            v2_rows = [v2s[b][hr, :] for b in range(PEER_TOPK)]
            tau_h = tau[hr, :]
            c1 = _prefix_count(v2_rows, lambda row: s1 + row >= tau_h)
            r2 = _prefix_count(v2_rows, lambda row: row > s2)
            p1 = jnp.exp(s1 - v1s[0][hr, :]) * inv_z[hr, :]
            r2 = r2.astype(BF16)
            p2 = jnp.exp(s2 - v2s[0][hr, :]).astype(BF16)
            for lt in range(n_lt):
                ls = slice(lt * LANES, (lt + 1) * LANES)
                for kb in range(nk // SUBLANES):
                    rs = slice(kb * SUBLANES, (kb + 1) * SUBLANES)
                    k1_scr[kb, lt, h, 0] = c1[rs, ls]
                    k1_scr[kb, lt, h, 1] = p1[rs, ls]
                for jb in range(nk // PACKED_ROWS):
                    rs = slice(jb * PACKED_ROWS, (jb + 1) * PACKED_ROWS)
                    k2_scr[jb, lt, h, 0] = r2[rs, ls]
                    k2_scr[jb, lt, h, 1] = p2[rs, ls]
        acc_scr[...] = jnp.zeros_like(acc_scr)
        for ref in (ga_a, ga_b, w_a, w_b):
            ref[...] = jnp.zeros_like(ref)

    chunk1 = jnp.clip(c - 1, 0, n_chunks - 1)
    zero = jnp.zeros((PACKED_ROWS, LANES), BF16)

    def gate_tile(e, lt, ga_old, w_new):
        blk = chunk1 * (n_e1 // SUBLANES) + e // SUBLANES
        r = e % SUBLANES
        ls = slice(lt * LANES, (lt + 1) * LANES)
        c1b, p1b = [], []
        for h in range(PEER_HEADS):
            c1_row = k1_scr[blk, lt, h, 0][r:r + 1, :]
            p1_row = k1_scr[blk, lt, h, 1][r:r + 1, :]
            c1b.append(jnp.broadcast_to(c1_row, (PACKED_ROWS, LANES)).astype(BF16))
            p1b.append(jnp.broadcast_to(p1_row, (PACKED_ROWS, LANES)).astype(BF16))
        for jb in range(nk // PACKED_ROWS):
            g = zero
            for h in range(PEER_HEADS):
                g = g + p1b[h] * jnp.where(k2_scr[jb, lt, h, 0] < c1b[h], k2_scr[jb, lt, h, 1], zero)
            ws = slice(e * nk + jb * PACKED_ROWS, e * nk + (jb + 1) * PACKED_ROWS)
            w_new[ws, ls] = g * ga_old[ws, ls]

    def tick(ga_new, ga_old, w_new, w_old):
        e_sub = 2
        n_q = 8
        dm = vt_ref.shape[0]
        qrows = dm // n_q
        tiles = [(eo, lt) for eo in range(e_sub) for lt in range(n_lt)]
        per_q = len(tiles) // n_q
        parts = [None] * n_q
        for s in range(n_e1 // e_sub):
            rows = slice(s * e_sub * nk, (s + 1) * e_sub * nk)
            for q in range(n_q):
                d = jnp.dot(vt_ref[q * qrows:(q + 1) * qrows, rows], w_old[rows, :],
                            preferred_element_type=F32)
                parts[q] = d if parts[q] is None else parts[q] + d
                for eo, lt in tiles[q * per_q:(q + 1) * per_q]:
                    gate_tile(s * e_sub + eo, lt, ga_old, w_new)
            act = jnp.dot(u_ref[rows, :].astype(BF16), xt_scr[...], preferred_element_type=F32)
            ga_new[rows, :] = jax.nn.gelu(act.astype(BF16))
        for q in range(n_q):
            acc_scr[q * qrows:(q + 1) * qrows, :] += parts[q]

    @pl.when(c % 2 == 0)
    def _():
        tick(ga_a, ga_b, w_b, w_a)

    @pl.when(c % 2 == 1)
    def _():
        tick(ga_b, ga_a, w_a, w_b)

    @pl.when(c == pl.num_programs(1) - 1)
    def _():
        o_ref[...] = acc_scr[...].T


def _peer(x, wq, k1, k2, u_all, v_all, layer, tn=512, n_e1=8):
    assert n_e1 % SUBLANES == 0
    t, dm = x.shape
    n_exp = u_all.shape[1]
    ce = n_e1 * PEER_NKEYS
    n_chunks = n_exp // ce
    hq = wq.shape[1]
    vt_tab = v_all[layer].reshape(n_chunks, ce, dm).transpose(0, 2, 1).astype(BF16)
    n_lt = tn // LANES
    k1_tab = pltpu.VMEM((PEER_NKEYS // SUBLANES, n_lt, PEER_HEADS, 2, SUBLANES, LANES), F32)
    k2_tab = pltpu.VMEM((PEER_NKEYS // PACKED_ROWS, n_lt, PEER_HEADS, 2, PACKED_ROWS, LANES), BF16)
    last = n_chunks - 1
    return pl.pallas_call(
        functools.partial(_peer_kernel, n_e1, n_chunks),
        grid=(t // tn, n_chunks + 2),
        in_specs=[
            pl.BlockSpec((tn, dm), lambda i, c: (i, 0)),
            pl.BlockSpec((dm, hq), lambda i, c: (0, 0)),
            pl.BlockSpec((PEER_NKEYS, PEER_NKEYS), lambda i, c: (0, 0)),
            pl.BlockSpec((PEER_NKEYS, PEER_NKEYS), lambda i, c: (0, 0)),
            pl.BlockSpec((None, ce, dm), lambda i, c: (layer, jnp.minimum(c, last), 0)),
            pl.BlockSpec((None, dm, ce), lambda i, c: (jnp.clip(c - 2, 0, last), 0, 0)),
        ],
        out_specs=pl.BlockSpec((tn, dm), lambda i, c: (i, 0)),
        out_shape=jax.ShapeDtypeStruct((t, dm), F32),
        scratch_shapes=[
            pltpu.VMEM((dm, tn), BF16),
            k1_tab, k2_tab,
            pltpu.VMEM((ce, tn), BF16), pltpu.VMEM((ce, tn), BF16),
            pltpu.VMEM((ce, tn), BF16), pltpu.VMEM((ce, tn), BF16),
            pltpu.VMEM((dm, tn), F32),
        ],
        compiler_params=_params("parallel", "arbitrary"),
        name="peer",
    )(x, wq, k1, k2, u_all, vt_tab)


@jax.jit
def _trunk(x, p, ln_mix_g, ln_mix_b, ln_ffn_g, ln_ffn_b, ssm_w_in, ssm_lam_re, ssm_lam_im,
           ssm_log_dt, ssm_b_re, ssm_b_im, ssm_c_re, ssm_c_im, ssm_d, ssm_w_glu,
           kv_ln_g, kv_ln_b, w_kv, sb_w_q, sb_w_o, peer_w_q, peer_k1, peer_k2,
           peer_u, peer_v, ple_w, ple_gate):
    bsz, seq, dm = x.shape
    depth = p.shape[0]
    n_a = ssm_w_in.shape[0]
    alpha = (2 * depth) ** 0.25
    t = bsz * seq
    h = x.reshape(t, dm)
    if n_a > 0:
        h = _to_chunk_order(h, bsz)
    kv = None
    for i in range(depth):
        p_i = p[i].reshape(t, -1)
        if i < n_a:
            p_i = _to_chunk_order(p_i, bsz)
            u = _mm(h, ssm_w_in[i].astype(BF16))
            z = _s5_gelu(u, ssm_d[i], bsz, ssm_lam_re[i], ssm_lam_im[i], ssm_log_dt[i],
                         ssm_b_re[i], ssm_b_im[i], ssm_c_re[i], ssm_c_im[i])
            wglu = ssm_w_glu[i].astype(BF16)
            h = _glu_ln(alpha, z, wglu[:, :dm], wglu[:, dm:], h, ln_mix_g[i], ln_mix_b[i])
        else:
            if i == n_a:
                kv = _ln_mm(h, kv_ln_g, kv_ln_b, w_kv.astype(BF16), out_dtype=BF16)
            j = i - n_a
            q = _mm(h, sb_w_q[j].astype(BF16), out_dtype=BF16)
            o = _stick_breaking(q, kv, bsz)
            h = _proj_ln(alpha, o, sb_w_o[j].astype(BF16), h, ln_mix_g[i], ln_mix_b[i])
        ffn = _peer(h, peer_w_q[i].astype(BF16), peer_k1[i], peer_k2[i],
                    peer_u, peer_v, i)
        h = _ffn_ln_ple(alpha, h, ffn, ln_ffn_g[i], ln_ffn_b[i], ple_gate[i].astype(BF16),
                        p_i, ple_w[i].astype(BF16))
        if i == n_a - 1:
            h = _from_chunk_order(h, bsz)
    return h.reshape(bsz, seq, dm)


def kernel(x, p, ln_mix_g, ln_mix_b, ln_ffn_g, ln_ffn_b, ssm_w_in, ssm_lam_re, ssm_lam_im, ssm_log_dt, ssm_b_re, ssm_b_im, ssm_c_re, ssm_c_im, ssm_d, ssm_w_glu, kv_ln_g, kv_ln_b, w_kv, sb_w_q, sb_w_o, peer_w_q, peer_k1, peer_k2, peer_u, peer_v, ple_w, ple_gate):
    return _trunk(x, p, ln_mix_g, ln_mix_b, ln_ffn_g, ln_ffn_b, ssm_w_in, ssm_lam_re, ssm_lam_im,
                  ssm_log_dt, ssm_b_re, ssm_b_im, ssm_c_re, ssm_c_im, ssm_d, ssm_w_glu,
                  kv_ln_g, kv_ln_b, w_kv, sb_w_q, sb_w_o, peer_w_q, peer_k1, peer_k2,
                  peer_u, peer_v, ple_w, ple_gate)
```

```python
import functools
import math

import jax
import jax.numpy as jnp
from jax import lax
from jax.experimental import pallas as pl
from jax.experimental.pallas import tpu as pltpu

F32 = jnp.float32
BF16 = jnp.bfloat16

HEAD_DIM = 64
SSM_GROUP = 16
SSM_STATE = 64
SSM_CHUNK = 16
PEER_HEADS = 8
PEER_NKEYS = 128
PEER_TOPK = 16
LN_EPS = 1e-5

LANES = 128
SUBLANES = 8
PACKED_ROWS = 16
VMEM_LIMIT = 56 * 1024 * 1024

_NT = (((1,), (1,)), ((), ()))


def _params(*sem, flags=None):
    return pltpu.CompilerParams(dimension_semantics=sem, vmem_limit_bytes=VMEM_LIMIT, flags=flags)


def _ln(x, g, b):
    mu = jnp.mean(x, axis=-1, keepdims=True)
    xc = x - mu
    var = jnp.mean(xc * xc, axis=-1, keepdims=True)
    return xc * lax.rsqrt(var + LN_EPS) * g + b


def _sigmoid(x):
    return 1.0 / (1.0 + jnp.exp(-x))


def _row_spec(tm, n):
    return pl.BlockSpec((tm, n), lambda i: (i, 0))


def _full_spec(shape):
    nd = len(shape)
    return pl.BlockSpec(shape, lambda i: (0,) * nd)


def _mm_kernel(x_ref, w_ref, o_ref):
    acc = jnp.dot(x_ref[...].astype(BF16), w_ref[...], preferred_element_type=F32)
    o_ref[...] = acc.astype(o_ref.dtype)


def _mm(x, w, out_dtype=F32, tm=512):
    t, k = x.shape
    n = w.shape[1]
    return pl.pallas_call(
        _mm_kernel,
        grid=(t // tm,),
        in_specs=[_row_spec(tm, k), _full_spec((k, n))],
        out_specs=_row_spec(tm, n),
        out_shape=jax.ShapeDtypeStruct((t, n), out_dtype),
        compiler_params=_params("parallel"),
        name="mm",
    )(x, w)


def _ln_mm_kernel(x_ref, g_ref, b_ref, w_ref, o_ref):
    y = _ln(x_ref[...], g_ref[...], b_ref[...])
    acc = jnp.dot(y.astype(BF16), w_ref[...], preferred_element_type=F32)
    o_ref[...] = acc.astype(o_ref.dtype)


def _ln_mm(x, g, b, w, out_dtype=F32, tm=512):
    t, k = x.shape
    n = w.shape[1]
    return pl.pallas_call(
        _ln_mm_kernel,
        grid=(t // tm,),
        in_specs=[_row_spec(tm, k), _full_spec((1, k)), _full_spec((1, k)), _full_spec((k, n))],
        out_specs=_row_spec(tm, n),
        out_shape=jax.ShapeDtypeStruct((t, n), out_dtype),
        compiler_params=_params("parallel"),
        name="ln_mm",
    )(x, g.reshape(1, k), b.reshape(1, k), w)


def _glu_ln_kernel(alpha, z_ref, wv_ref, wg_ref, h_ref, g_ref, b_ref, o_ref):
    z = z_ref[...]
    val = jnp.dot(z, wv_ref[...], preferred_element_type=F32)
    gate = jnp.dot(z, wg_ref[...], preferred_element_type=F32)
    mix = val * _sigmoid(gate)
    o_ref[...] = _ln(alpha * h_ref[...] + mix, g_ref[...], b_ref[...])


def _glu_ln(alpha, z, wv, wg, h, g, b, tm=512):
    t, dm = h.shape
    row = _row_spec(tm, dm)
    vec = _full_spec((1, dm))
    return pl.pallas_call(
        functools.partial(_glu_ln_kernel, alpha),
        grid=(t // tm,),
        in_specs=[row, _full_spec((dm, dm)), _full_spec((dm, dm)), row, vec, vec],
        out_specs=row,
        out_shape=jax.ShapeDtypeStruct((t, dm), F32),
        compiler_params=_params("parallel"),
        name="glu_ln",
    )(z, wv, wg, h, g.reshape(1, dm), b.reshape(1, dm))


def _proj_ln_kernel(alpha, o_in_ref, w_ref, h_ref, g_ref, b_ref, o_ref):
    mix = jnp.dot(o_in_ref[...], w_ref[...], preferred_element_type=F32)
    o_ref[...] = _ln(alpha * h_ref[...] + mix, g_ref[...], b_ref[...])


def _proj_ln(alpha, o, w, h, g, b, tm=512):
    t, dm = h.shape
    row = _row_spec(tm, dm)
    vec = _full_spec((1, dm))
    return pl.pallas_call(
        functools.partial(_proj_ln_kernel, alpha),
        grid=(t // tm,),
        in_specs=[row, _full_spec((dm, dm)), row, vec, vec],
        out_specs=row,
        out_shape=jax.ShapeDtypeStruct((t, dm), F32),
        compiler_params=_params("parallel"),
        name="proj_ln",
    )(o, w, h, g.reshape(1, dm), b.reshape(1, dm))


def _ffn_ln_ple_kernel(alpha, h_ref, f_ref, g_ref, b_ref, wg_ref, p_ref, wp_ref, o_ref):
    h2 = _ln(alpha * h_ref[...] + f_ref[...], g_ref[...], b_ref[...])
    gate = jnp.dot(h2.astype(BF16), wg_ref[...], preferred_element_type=F32)
    emb = jnp.dot(p_ref[...].astype(BF16), wp_ref[...], preferred_element_type=F32)
    o_ref[...] = h2 + _sigmoid(gate) * emb


def _ffn_ln_ple(alpha, h, ffn, g, b, w_gate, p, w_ple, tm=512):
    t, dm = h.shape
    pd = p.shape[1]
    row = _row_spec(tm, dm)
    vec = _full_spec((1, dm))
    return pl.pallas_call(
        functools.partial(_ffn_ln_ple_kernel, alpha),
        grid=(t // tm,),
        in_specs=[row, row, vec, vec, _full_spec((dm, dm)), _row_spec(tm, pd), _full_spec((pd, dm))],
        out_specs=row,
        out_shape=jax.ShapeDtypeStruct((t, dm), F32),
        compiler_params=_params("parallel"),
        name="ffn_ln_ple",
    )(h, ffn, g.reshape(1, dm), b.reshape(1, dm), w_gate, p, w_ple)


def _s5_tables(lam_re, lam_im, log_dt, b_re, b_im, c_re, c_im):
    ell = SSM_CHUNK
    g, p = lam_re.shape
    dt = jnp.exp(log_dt)[:, None]

    def apow(k):
        mag = jnp.exp(k * lam_re * dt)
        ang = k * lam_im * dt
        return mag * jnp.cos(ang), mag * jnp.sin(ang)

    a_re, a_im = apow(1.0)
    den = lam_re * lam_re + lam_im * lam_im
    f_re = ((a_re - 1.0) * lam_re + a_im * lam_im) / den
    f_im = (a_im * lam_re - (a_re - 1.0) * lam_im) / den
    bb_re = f_re[..., None] * b_re - f_im[..., None] * b_im
    bb_im = f_re[..., None] * b_im + f_im[..., None] * b_re

    ks = jnp.arange(ell + 1, dtype=F32)[:, None, None]
    pw_re, pw_im = apow(ks)

    ca_re = c_re[None] * pw_re[:, :, None, :] - c_im[None] * pw_im[:, :, None, :]
    ca_im = c_re[None] * pw_im[:, :, None, :] + c_im[None] * pw_re[:, :, None, :]
    kk = (jnp.einsum('kgcp,gpd->kgcd', ca_re[:ell], bb_re)
          - jnp.einsum('kgcp,gpd->kgcd', ca_im[:ell], bb_im))

    rp_re, rp_im = pw_re[:ell][::-1], pw_im[:ell][::-1]
    ps_re = rp_re[..., None] * bb_re[None] - rp_im[..., None] * bb_im[None]
    ps_im = rp_re[..., None] * bb_im[None] + rp_im[..., None] * bb_re[None]
    pm = jnp.stack([ps_re, ps_im], axis=2)
    pm = pm.transpose(1, 0, 4, 2, 3).reshape(g, ell * SSM_GROUP, 2, p)

    q = jnp.stack([ca_re[1:], -ca_im[1:]], axis=3)
    q = q.transpose(1, 3, 4, 0, 2).reshape(g, 2, p, ell * SSM_GROUP)
    return kk, pm, q


def _s5_block_tables(lam_re, lam_im, log_dt, b_re, b_im, c_re, c_im, n_steps):
    ell, gc, p = SSM_CHUNK, SSM_GROUP, SSM_STATE
    g = lam_re.shape[0]
    gb = LANES // gc
    nb = g // gb
    kk, pm, q = _s5_tables(lam_re, lam_im, log_dt, b_re, b_im, c_re, c_im)
    same = jnp.eye(gb, dtype=bool)
    zero = jnp.zeros((), BF16)
    k6 = kk.astype(BF16).reshape(ell, nb, gb, gc, gc).transpose(1, 0, 2, 4, 3)
    k_blk = jnp.where(same[None, None, :, None, :, None], k6[:, :, :, :, None, :], zero)
    k_blk = k_blk.reshape(nb, ell, LANES, LANES)
    kz = jnp.concatenate([jnp.zeros((nb, 1, LANES, LANES), BF16), k_blk], axis=1)
    half = ell // 2
    top = jnp.concatenate([kz[:, 1::2], kz[:, 2::2]], axis=-1)
    bottom = jnp.concatenate([kz[:, 0::2][:, :half], kz[:, 1::2]], axis=-1)
    w_tab = jnp.concatenate([top, bottom], axis=-2)
    p6 = pm.astype(BF16).reshape(nb, gb, ell, gc, 2, p).transpose(0, 2, 1, 3, 4, 5)
    p_big = jnp.where(same[None, None, :, None, None, :, None], p6[:, :, :, :, :, None, :], zero)
    p_big = p_big.reshape(nb, ell * LANES, 2 * gb * p)
    q6 = q.astype(BF16).reshape(nb, gb, 2, p, ell, gc).transpose(0, 2, 1, 3, 4, 5)
    q_big = jnp.where(same[None, None, :, None, None, :, None], q6[:, :, :, :, :, None, :], zero)
    q_big = q_big.reshape(nb, 2 * gb * p, ell * LANES)
    dt = jnp.exp(log_dt)[:, None]
    ks = (ell * (2.0 ** jnp.arange(n_steps, dtype=F32)))[None, :, None]
    mag = jnp.exp(ks * (lam_re * dt)[:, None, :])
    ang = ks * (lam_im * dt)[:, None, :]
    are = (mag * jnp.cos(ang)).reshape(nb, gb, n_steps, p).transpose(0, 2, 1, 3).reshape(nb, n_steps, gb * p)
    aim = (mag * jnp.sin(ang)).reshape(nb, gb, n_steps, p).transpose(0, 2, 1, 3).reshape(nb, n_steps, gb * p)
    are2 = jnp.concatenate([are, are], axis=-1)
    aim2 = jnp.concatenate([-aim, aim], axis=-1)
    return w_tab, p_big, q_big, are2, aim2


def _s5_kernel(n_chunks, n_steps, u_ref, d_ref, w_ref, p_ref, q_ref, are_ref, aim_ref, z_ref):
    ell = SSM_CHUNK
    half = ell // 2
    xs = [jnp.concatenate([u_ref[j * n_chunks:(j + 1) * n_chunks, :].astype(BF16)
                           for j in (2 * jp, 2 * jp + 1)], axis=1) for jp in range(half)]
    xcat = jnp.concatenate(xs, axis=1)
    x = jnp.dot(xcat, p_ref[0], preferred_element_type=F32)
    width = x.shape[1]
    n_idx = lax.broadcasted_iota(jnp.int32, x.shape, 0)
    for s in range(n_steps):
        sh = 1 << s
        prev = jnp.where(n_idx >= sh, pltpu.roll(x, sh, 0), 0.0)
        swapped = pltpu.roll(prev, width // 2, 1)
        x = x + are_ref[0, s:s + 1, :] * prev + aim_ref[0, s:s + 1, :] * swapped
    x_start = jnp.where(n_idx >= 1, pltpu.roll(x, 1, 0), 0.0)
    y_state = jnp.dot(x_start.astype(BF16), q_ref[0], preferred_element_type=F32)
    for tp in range(half):
        y = y_state[:, 2 * tp * LANES:2 * (tp + 1) * LANES]
        for jp in range(tp + 1):
            y = y + jnp.dot(xs[jp], w_ref[0, tp - jp], preferred_element_type=F32)
        for tl in range(2):
            rows = slice((2 * tp + tl) * n_chunks, (2 * tp + tl + 1) * n_chunks)
            zj = jax.nn.gelu(y[:, tl * LANES:(tl + 1) * LANES] + d_ref[...] * u_ref[rows, :])
            z_ref[rows, :] = zj.astype(z_ref.dtype)


def _s5_gelu(u, d, bsz, lam_re, lam_im, log_dt, b_re, b_im, c_re, c_im):
    t, dm = u.shape
    seq = t // bsz
    n_chunks = seq // SSM_CHUNK
    assert n_chunks & (n_chunks - 1) == 0, "chunks per sequence must be a power of two"
    n_steps = max(1, (n_chunks - 1).bit_length())
    w_tab, p_big, q_big, are2, aim2 = _s5_block_tables(
        lam_re, lam_im, log_dt, b_re, b_im, c_re, c_im, n_steps)
    nb = dm // LANES
    kw = SSM_CHUNK * LANES
    sw = p_big.shape[2]
    half = SSM_CHUNK // 2
    return pl.pallas_call(
        functools.partial(_s5_kernel, n_chunks, n_steps),
        grid=(nb, bsz),
        in_specs=[
            pl.BlockSpec((seq, LANES), lambda i, b: (b, i)),
            pl.BlockSpec((1, LANES), lambda i, b: (0, i)),
            pl.BlockSpec((1, half, 2 * LANES, 2 * LANES), lambda i, b: (i, 0, 0, 0)),
            pl.BlockSpec((1, kw, sw), lambda i, b: (i, 0, 0)),
            pl.BlockSpec((1, sw, kw), lambda i, b: (i, 0, 0)),
            pl.BlockSpec((1, n_steps, sw), lambda i, b: (i, 0, 0)),
            pl.BlockSpec((1, n_steps, sw), lambda i, b: (i, 0, 0)),
        ],
        out_specs=pl.BlockSpec((seq, LANES), lambda i, b: (b, i)),
        out_shape=jax.ShapeDtypeStruct((t, dm), BF16),
        compiler_params=_params("parallel", "arbitrary"),
        name="s5_gelu",
    )(u, d.reshape(1, dm), w_tab, p_big, q_big, are2, aim2)


def _to_chunk_order(a, bsz):
    t, f = a.shape
    n = t // (bsz * SSM_CHUNK)
    return a.reshape(bsz, n, SSM_CHUNK, f).transpose(0, 2, 1, 3).reshape(t, f)


def _from_chunk_order(a, bsz):
    t, f = a.shape
    n = t // (bsz * SSM_CHUNK)
    return a.reshape(bsz, SSM_CHUNK, n, f).transpose(0, 2, 1, 3).reshape(t, f)


def _sb_pointwise(z2):
    pos = jnp.maximum(z2, 0.0)
    neg = z2 - pos
    soft = jnp.log2(1.0 + jnp.exp2(neg - pos))
    return pos + soft, neg - soft


def _sb_scores(qh, kb, tri, diag_mask):
    z2 = lax.dot_general(qh, kb, _NT, preferred_element_type=F32)
    neg_log_keep, log_beta = _sb_pointwise(z2)
    if diag_mask is not None:
        neg_log_keep = jnp.where(diag_mask, neg_log_keep, 0.0)
    later = jnp.dot(neg_log_keep.astype(BF16), tri, preferred_element_type=F32)
    arg = log_beta - later
    if diag_mask is not None:
        arg = jnp.where(diag_mask, arg, -jnp.inf)
    return arg, jnp.sum(neg_log_keep, axis=1, keepdims=True)


def _sb_kernel(bq, scale, q_ref, k_ref, v_ref, o_ref, arg_scr, acc_scr):
    qi = pl.program_id(2)
    n_heads = q_ref.shape[2] // HEAD_DIM
    row = lax.broadcasted_iota(jnp.int32, (bq, bq), 0)
    col = lax.broadcasted_iota(jnp.int32, (bq, bq), 1)
    tri = (row > col).astype(BF16)
    diag_mask = col < row
    lanes = [slice(hh * HEAD_DIM, (hh + 1) * HEAD_DIM) for hh in range(n_heads)]
    qs = [(q_ref[0, :, ls].astype(F32) * scale).astype(BF16) for ls in lanes]

    def rows_of(ref, kblk, ls):
        return ref[0, pl.ds(pl.multiple_of(kblk * bq, bq), bq), ls]

    def finish(hh, carry, vblk):
        w = jnp.exp2(arg_scr[hh] - carry)
        return jnp.dot(w.astype(BF16), rows_of(v_ref, vblk, lanes[hh]), preferred_element_type=F32)

    acc_scr[...] = jnp.zeros_like(acc_scr)
    state = []
    for hh in range(n_heads):
        arg, rowsum = _sb_scores(qs[hh], rows_of(k_ref, qi, lanes[hh]), tri, diag_mask)
        arg_scr[hh] = arg
        state.extend((jnp.zeros((bq, 1), F32), rowsum))

    def body(i, state):
        new = [None] * (2 * n_heads)
        group = 4
        for g0 in range(0, n_heads, group):
            heads = range(g0, min(g0 + group, n_heads))
            z2 = {hh: lax.dot_general(qs[hh], rows_of(k_ref, qi - 1 - i, lanes[hh]), _NT,
                                      preferred_element_type=F32) for hh in heads}
            w = {hh: jnp.exp2(arg_scr[hh] - state[2 * hh]).astype(BF16) for hh in heads}
            point = {hh: _sb_pointwise(z2[hh]) for hh in heads}
            for hh in heads:
                acc_scr[:, lanes[hh]] += jnp.dot(w[hh], rows_of(v_ref, qi - i, lanes[hh]),
                                                 preferred_element_type=F32)
            later = {hh: jnp.dot(point[hh][0].astype(BF16), tri, preferred_element_type=F32)
                     for hh in heads}
            for hh in heads:
                arg_scr[hh] = point[hh][1] - later[hh]
                new[2 * hh] = state[2 * hh] + state[2 * hh + 1]
                new[2 * hh + 1] = jnp.sum(point[hh][0], axis=1, keepdims=True)
        return tuple(new)

    state = lax.fori_loop(0, qi, body, tuple(state))
    outs = [acc_scr[:, lanes[hh]] + finish(hh, state[2 * hh], 0) for hh in range(n_heads)]
    o_ref[0] = jnp.concatenate(outs, axis=1).astype(o_ref.dtype)


def _stick_breaking(q, kv, bsz, bq=256, heads_per_step=8):
    t, dm = q.shape
    seq = t // bsz
    hw = heads_per_step * HEAD_DIM
    n_hb = dm // hw
    q3 = q.reshape(bsz, seq, dm)
    kv3 = kv.reshape(bsz, seq, 2 * dm)
    out = pl.pallas_call(
        functools.partial(_sb_kernel, bq, HEAD_DIM ** -0.5 * math.log2(math.e)),
        grid=(bsz, n_hb, seq // bq),
        in_specs=[
            pl.BlockSpec((1, bq, hw), lambda b, h, i: (b, i, h)),
            pl.BlockSpec((1, seq, hw), lambda b, h, i: (b, 0, h)),
            pl.BlockSpec((1, seq, hw), lambda b, h, i: (b, 0, n_hb + h)),
        ],
        out_specs=pl.BlockSpec((1, bq, hw), lambda b, h, i: (b, i, h)),
        out_shape=jax.ShapeDtypeStruct((bsz, seq, dm), BF16),
        scratch_shapes=[pltpu.VMEM((heads_per_step, bq, bq), F32), pltpu.VMEM((bq, hw), F32)],
        compiler_params=_params("parallel", "parallel", "arbitrary"),
        name="stick_breaking",
    )(q3, kv3, kv3)
    return out.reshape(t, dm)


def _cmpx(a, i, j):
    hi = jnp.maximum(a[i], a[j])
    lo = jnp.minimum(a[i], a[j])
    a[i], a[j] = hi, lo


def _bitonic_merge_desc(a):
    n = len(a)
    j = n // 2
    while j >= 1:
        for i in range(n):
            l = i ^ j
            if l > i:
                _cmpx(a, i, l)
        j //= 2


def _sorted_top16(s):
    k = PEER_TOPK
    a = [s[SUBLANES * i:SUBLANES * (i + 1), :] for i in range(k)]
    size = 2
    while size <= k:
        j = size // 2
        while j >= 1:
            for i in range(k):
                l = i ^ j
                if l > i:
                    if (i & size) == 0:
                        _cmpx(a, i, l)
                    else:
                        _cmpx(a, l, i)
            j //= 2
        size *= 2
    for shift in (4, 2, 1):
        b = [pltpu.roll(x, shift, 0) for x in a]
        a = [jnp.maximum(a[i], b[k - 1 - i]) for i in range(k)]
        _bitonic_merge_desc(a)
    return a


def _peer_select(v1, v2):
    k = PEER_TOPK
    cells = [v1[a] + v2[b] for a in range(k) for b in range(k) if (a + 1) * (b + 1) <= k]
    top = cells[0]
    cur = list(cells)
    remaining = jnp.full(top.shape, float(k), F32)
    tau = top
    neg = jnp.float32(-jnp.inf)
    for _ in range(k):
        m = functools.reduce(jnp.maximum, cur)
        eq = [c == m for c in cur]
        cnt = functools.reduce(jnp.add, [e.astype(F32) for e in eq])
        tau = jnp.where(remaining > 0.0, m, tau)
        remaining = remaining - cnt
        cur = [jnp.where(e, neg, c) for e, c in zip(eq, cur)]
    zsum = functools.reduce(
        jnp.add, [jnp.where(c >= tau, jnp.exp(c - top), 0.0) for c in cells])
    return tau, zsum


def _prefix_count(vals, test):
    m8 = test(vals[7])
    m4 = test(jnp.where(m8, vals[11], vals[3]))
    m2 = test(jnp.where(m8, jnp.where(m4, vals[13], vals[9]), jnp.where(m4, vals[5], vals[1])))
    hi = jnp.where(m4, jnp.where(m2, vals[14], vals[12]), jnp.where(m2, vals[10], vals[8]))
    lo = jnp.where(m4, jnp.where(m2, vals[6], vals[4]), jnp.where(m2, vals[2], vals[0]))
    m1 = test(jnp.where(m8, hi, lo))
    m16 = test(vals[15])
    count = jnp.where(m8, 8.0, 0.0) + jnp.where(m4, 4.0, 0.0) + jnp.where(m2, 2.0, 0.0)
    return count + jnp.where(m1, 1.0, 0.0) + jnp.where(m16, 1.0, 0.0)


def _split_bf16(x):
    hi = x.astype(BF16)
    lo = (x - hi.astype(F32)).astype(BF16)
    return hi, lo


def _scores_t(k_hi, k_lo, qh):
    q_hi, q_lo = _split_bf16(qh)
    s = lax.dot_general(k_hi, q_hi, _NT, preferred_element_type=F32)
    s = s + lax.dot_general(k_hi, q_lo, _NT, preferred_element_type=F32)
    return s + lax.dot_general(k_lo, q_hi, _NT, preferred_element_type=F32)


def _peer_kernel(n_e1, n_chunks, x_ref, wq_ref, k1_ref, k2_ref, u_ref, vt_ref, o_ref,
                 xt_scr, k1_scr, k2_scr, ga_a, ga_b, w_a, w_b, acc_scr):
    c = pl.program_id(1)
    tn = x_ref.shape[0]
    nk = PEER_NKEYS
    n_lt = tn // LANES

    @pl.when(c == 0)
    def _():
        xb = x_ref[...].astype(BF16)
        xt_scr[...] = x_ref[...].T.astype(BF16)
        q = jnp.dot(xb, wq_ref[...], preferred_element_type=F32)
        k1_hi, k1_lo = _split_bf16(k1_ref[...])
        k2_hi, k2_lo = _split_bf16(k2_ref[...])
        sub = lax.broadcasted_iota(jnp.int32, (SUBLANES, tn), 0)
        v1s = [jnp.zeros((SUBLANES, tn), F32) for _ in range(PEER_TOPK)]
        v2s = [jnp.zeros((SUBLANES, tn), F32) for _ in range(PEER_TOPK)]
        s1_all, s2_all = [], []
        for h in range(PEER_HEADS):
            s1 = _scores_t(k1_hi, k1_lo, q[:, 2 * nk * h:2 * nk * h + nk])
            s2 = _scores_t(k2_hi, k2_lo, q[:, 2 * nk * h + nk:2 * nk * (h + 1)])
            s1_all.append(s1)
            s2_all.append(s2)
            t1 = _sorted_top16(s1)
            t2 = _sorted_top16(s2)
            for a in range(PEER_TOPK):
                v1s[a] = jnp.where(sub == h, t1[a], v1s[a])
                v2s[a] = jnp.where(sub == h, t2[a], v2s[a])
        tau, zsum = _peer_select(v1s, v2s)
        inv_z = 1.0 / zsum
        for h in range(PEER_HEADS):
            s1, s2 = s1_all[h], s2_all[h]
            hr = slice(h, h + 1)
            v2_rows = [v2s[b][hr, :] for b in range(PEER_TOPK)]
            tau_h = tau[hr, :]
            c1 = _prefix_count(v2_rows, lambda row: s1 + row >= tau_h)
            r2 = _prefix_count(v2_rows, lambda row: row > s2)
            p1 = jnp.exp(s1 - v1s[0][hr, :]) * inv_z[hr, :]
            r2 = r2.astype(BF16)
            p2 = jnp.exp(s2 - v2s[0][hr, :]).astype(BF16)
            for lt in range(n_lt):
                ls = slice(lt * LANES, (lt + 1) * LANES)
                for kb in range(nk // SUBLANES):
                    rs = slice(kb * SUBLANES, (kb + 1) * SUBLANES)
                    k1_scr[kb, lt, h, 0] = c1[rs, ls]
                    k1_scr[kb, lt, h, 1] = p1[rs, ls]
                for jb in range(nk // PACKED_ROWS):
                    rs = slice(jb * PACKED_ROWS, (jb + 1) * PACKED_ROWS)
                    k2_scr[jb, lt, h, 0] = r2[rs, ls]
                    k2_scr[jb, lt, h, 1] = p2[rs, ls]
        acc_scr[...] = jnp.zeros_like(acc_scr)
        w_b[...] = jnp.zeros_like(w_b)
        act = jnp.dot(u_ref[...].astype(BF16), xt_scr[...], preferred_element_type=F32)
        ga_a[...] = jax.nn.gelu(act.astype(BF16))

    chunk1 = jnp.clip(c - 1, 0, n_chunks - 1)
    zero = jnp.zeros((PACKED_ROWS, LANES), BF16)

    def gate_tile(e, lt, ga_old, w_new):
        blk = chunk1 * (n_e1 // SUBLANES) + e // SUBLANES
        r = e % SUBLANES
        ls = slice(lt * LANES, (lt + 1) * LANES)
        c1b, p1b = [], []
        for h in range(PEER_HEADS):
            c1_row = k1_scr[blk, lt, h, 0][r:r + 1, :]
            p1_row = k1_scr[blk, lt, h, 1][r:r + 1, :]
            c1b.append(jnp.broadcast_to(c1_row, (PACKED_ROWS, LANES)).astype(BF16))
            p1b.append(jnp.broadcast_to(p1_row, (PACKED_ROWS, LANES)).astype(BF16))
        for jb in range(nk // PACKED_ROWS):
            g = zero
            for h in range(PEER_HEADS):
                g = g + p1b[h] * jnp.where(k2_scr[jb, lt, h, 0] < c1b[h], k2_scr[jb, lt, h, 1], zero)
            ws = slice(e * nk + jb * PACKED_ROWS, e * nk + (jb + 1) * PACKED_ROWS)
            w_new[ws, ls] = g * ga_old[ws, ls]

    def tick(ga_new, ga_old, w_new, w_old):
        e_sub = 2
        n_q = 8
        dm = vt_ref.shape[0]
        qrows = dm // n_q
        tiles = [(eo, lt) for eo in range(e_sub) for lt in range(n_lt)]
        per_q = len(tiles) // n_q
        parts = [None] * n_q
        for s in range(n_e1 // e_sub):
            rows = slice(s * e_sub * nk, (s + 1) * e_sub * nk)
            for q in range(n_q):
                d = jnp.dot(vt_ref[q * qrows:(q + 1) * qrows, rows], w_old[rows, :],
                            preferred_element_type=F32)
                parts[q] = d if parts[q] is None else parts[q] + d
                for eo, lt in tiles[q * per_q:(q + 1) * per_q]:
                    gate_tile(s * e_sub + eo, lt, ga_old, w_new)
            act = jnp.dot(u_ref[rows, :].astype(BF16), xt_scr[...], preferred_element_type=F32)
            ga_new[rows, :] = jax.nn.gelu(act.astype(BF16))
        for q in range(n_q):
            acc_scr[q * qrows:(q + 1) * qrows, :] += parts[q]

    full = jnp.logical_and(c >= 1, c <= n_chunks)

    @pl.when(jnp.logical_and(full, c % 2 == 0))
    def _():
        tick(ga_a, ga_b, w_b, w_a)

    @pl.when(jnp.logical_and(full, c % 2 == 1))
    def _():
        tick(ga_b, ga_a, w_a, w_b)

    @pl.when(c == n_chunks + 1)
    def _():
        w_last = w_b if (n_chunks - 1) % 2 == 1 else w_a
        out_t = acc_scr[...] + jnp.dot(vt_ref[...], w_last[...], preferred_element_type=F32)
        o_ref[...] = out_t.T


def _peer(x, wq, k1, k2, u_all, v_all, layer, tn=512, n_e1=8):
    assert n_e1 % SUBLANES == 0
    t, dm = x.shape
    n_exp = u_all.shape[1]
    ce = n_e1 * PEER_NKEYS
    n_chunks = n_exp // ce
    hq = wq.shape[1]
    vt_tab = v_all[layer].reshape(n_chunks, ce, dm).transpose(0, 2, 1).astype(BF16)
    n_lt = tn // LANES
    k1_tab = pltpu.VMEM((PEER_NKEYS // SUBLANES, n_lt, PEER_HEADS, 2, SUBLANES, LANES), F32)
    k2_tab = pltpu.VMEM((PEER_NKEYS // PACKED_ROWS, n_lt, PEER_HEADS, 2, PACKED_ROWS, LANES), BF16)
    last = n_chunks - 1
    return pl.pallas_call(
        functools.partial(_peer_kernel, n_e1, n_chunks),
        grid=(t // tn, n_chunks + 2),
        in_specs=[
            pl.BlockSpec((tn, dm), lambda i, c: (i, 0)),
            pl.BlockSpec((dm, hq), lambda i, c: (0, 0)),
            pl.BlockSpec((PEER_NKEYS, PEER_NKEYS), lambda i, c: (0, 0)),
            pl.BlockSpec((PEER_NKEYS, PEER_NKEYS), lambda i, c: (0, 0)),
            pl.BlockSpec((None, ce, dm), lambda i, c: (layer, jnp.minimum(c, last), 0)),
            pl.BlockSpec((None, dm, ce), lambda i, c: (jnp.clip(c - 2, 0, last), 0, 0)),
        ],
        out_specs=pl.BlockSpec((tn, dm), lambda i, c: (i, 0)),
        out_shape=jax.ShapeDtypeStruct((t, dm), F32),
        scratch_shapes=[
            pltpu.VMEM((dm, tn), BF16),
            k1_tab, k2_tab,
            pltpu.VMEM((ce, tn), BF16), pltpu.VMEM((ce, tn), BF16),
            pltpu.VMEM((ce, tn), BF16), pltpu.VMEM((ce, tn), BF16),
            pltpu.VMEM((dm, tn), F32),
        ],
        compiler_params=_params("parallel", "arbitrary"),
        name="peer",
    )(x, wq, k1, k2, u_all, vt_tab)


@jax.jit
def _trunk(x, p, ln_mix_g, ln_mix_b, ln_ffn_g, ln_ffn_b, ssm_w_in, ssm_lam_re, ssm_lam_im,
           ssm_log_dt, ssm_b_re, ssm_b_im, ssm_c_re, ssm_c_im, ssm_d, ssm_w_glu,
           kv_ln_g, kv_ln_b, w_kv, sb_w_q, sb_w_o, peer_w_q, peer_k1, peer_k2,
           peer_u, peer_v, ple_w, ple_gate):
    bsz, seq, dm = x.shape
    depth = p.shape[0]
    n_a = ssm_w_in.shape[0]
    alpha = (2 * depth) ** 0.25
    t = bsz * seq
    h = x.reshape(t, dm)
    if n_a > 0:
        h = _to_chunk_order(h, bsz)
    kv = None
    for i in range(depth):
        p_i = p[i].reshape(t, -1)
        if i < n_a:
            p_i = _to_chunk_order(p_i, bsz)
            u = _mm(h, ssm_w_in[i].astype(BF16))
            z = _s5_gelu(u, ssm_d[i], bsz, ssm_lam_re[i], ssm_lam_im[i], ssm_log_dt[i],
                         ssm_b_re[i], ssm_b_im[i], ssm_c_re[i], ssm_c_im[i])
            wglu = ssm_w_glu[i].astype(BF16)
            h = _glu_ln(alpha, z, wglu[:, :dm], wglu[:, dm:], h, ln_mix_g[i], ln_mix_b[i])
        else:
            if i == n_a:
                kv = _ln_mm(h, kv_ln_g, kv_ln_b, w_kv.astype(BF16), out_dtype=BF16)
            j = i - n_a
            q = _mm(h, sb_w_q[j].astype(BF16), out_dtype=BF16)
            o = _stick_breaking(q, kv, bsz)
            h = _proj_ln(alpha, o, sb_w_o[j].astype(BF16), h, ln_mix_g[i], ln_mix_b[i])
        ffn = _peer(h, peer_w_q[i].astype(BF16), peer_k1[i], peer_k2[i],
                    peer_u, peer_v, i)
        h = _ffn_ln_ple(alpha, h, ffn, ln_ffn_g[i], ln_ffn_b[i], ple_gate[i].astype(BF16),
                        p_i, ple_w[i].astype(BF16))
        if i == n_a - 1:
            h = _from_chunk_order(h, bsz)
    return h.reshape(bsz, seq, dm)


def kernel(x, p, ln_mix_g, ln_mix_b, ln_ffn_g, ln_ffn_b, ssm_w_in, ssm_lam_re, ssm_lam_im, ssm_log_dt, ssm_b_re, ssm_b_im, ssm_c_re, ssm_c_im, ssm_d, ssm_w_glu, kv_ln_g, kv_ln_b, w_kv, sb_w_q, sb_w_o, peer_w_q, peer_k1, peer_k2, peer_u, peer_v, ple_w, ple_gate):
    return _trunk(x, p, ln_mix_g, ln_mix_b, ln_ffn_g, ln_ffn_b, ssm_w_in, ssm_lam_re, ssm_lam_im,
                  ssm_log_dt, ssm_b_re, ssm_b_im, ssm_c_re, ssm_c_im, ssm_d, ssm_w_glu,
                  kv_ln_g, kv_ln_b, w_kv, sb_w_q, sb_w_o, peer_w_q, peer_k1, peer_k2,
                  peer_u, peer_v, ple_w, ple_gate)
```

```python
import functools
import math

import jax
import jax.numpy as jnp
from jax import lax
from jax.experimental import pallas as pl
from jax.experimental.pallas import tpu as pltpu

F32 = jnp.float32
BF16 = jnp.bfloat16

HEAD_DIM = 64
SSM_GROUP = 16
SSM_STATE = 64
SSM_CHUNK = 16
PEER_HEADS = 8
PEER_NKEYS = 128
PEER_TOPK = 16
LN_EPS = 1e-5

LANES = 128
SUBLANES = 8
PACKED_ROWS = 16
VMEM_LIMIT = 56 * 1024 * 1024

_NT = (((1,), (1,)), ((), ()))


def _params(*sem, flags=None):
    return pltpu.CompilerParams(dimension_semantics=sem, vmem_limit_bytes=VMEM_LIMIT, flags=flags)


def _ln(x, g, b):
    mu = jnp.mean(x, axis=-1, keepdims=True)
    xc = x - mu
    var = jnp.mean(xc * xc, axis=-1, keepdims=True)
    return xc * lax.rsqrt(var + LN_EPS) * g + b


def _sigmoid(x):
    return 1.0 / (1.0 + jnp.exp(-x))


def _row_spec(tm, n):
    return pl.BlockSpec((tm, n), lambda i: (i, 0))


def _full_spec(shape):
    nd = len(shape)
    return pl.BlockSpec(shape, lambda i: (0,) * nd)


def _mm_kernel(x_ref, w_ref, o_ref):
    acc = jnp.dot(x_ref[...].astype(BF16), w_ref[...], preferred_element_type=F32)
    o_ref[...] = acc.astype(o_ref.dtype)


def _mm(x, w, out_dtype=F32, tm=512):
    t, k = x.shape
    n = w.shape[1]
    return pl.pallas_call(
        _mm_kernel,
        grid=(t // tm,),
        in_specs=[_row_spec(tm, k), _full_spec((k, n))],
        out_specs=_row_spec(tm, n),
        out_shape=jax.ShapeDtypeStruct((t, n), out_dtype),
        compiler_params=_params("parallel"),
        name="mm",
    )(x, w)


def _ln_mm_kernel(x_ref, g_ref, b_ref, w_ref, o_ref):
    y = _ln(x_ref[...], g_ref[...], b_ref[...])
    acc = jnp.dot(y.astype(BF16), w_ref[...], preferred_element_type=F32)
    o_ref[...] = acc.astype(o_ref.dtype)


def _ln_mm(x, g, b, w, out_dtype=F32, tm=512):
    t, k = x.shape
    n = w.shape[1]
    return pl.pallas_call(
        _ln_mm_kernel,
        grid=(t // tm,),
        in_specs=[_row_spec(tm, k), _full_spec((1, k)), _full_spec((1, k)), _full_spec((k, n))],
        out_specs=_row_spec(tm, n),
        out_shape=jax.ShapeDtypeStruct((t, n), out_dtype),
        compiler_params=_params("parallel"),
        name="ln_mm",
    )(x, g.reshape(1, k), b.reshape(1, k), w)


def _glu_ln_kernel(alpha, z_ref, wv_ref, wg_ref, h_ref, g_ref, b_ref, o_ref):
    z = z_ref[...]
    val = jnp.dot(z, wv_ref[...], preferred_element_type=F32)
    gate = jnp.dot(z, wg_ref[...], preferred_element_type=F32)
    mix = val * _sigmoid(gate)
    o_ref[...] = _ln(alpha * h_ref[...] + mix, g_ref[...], b_ref[...])


def _glu_ln(alpha, z, wv, wg, h, g, b, tm=512):
    t, dm = h.shape
    row = _row_spec(tm, dm)
    vec = _full_spec((1, dm))
    return pl.pallas_call(
        functools.partial(_glu_ln_kernel, alpha),
        grid=(t // tm,),
        in_specs=[row, _full_spec((dm, dm)), _full_spec((dm, dm)), row, vec, vec],
        out_specs=row,
        out_shape=jax.ShapeDtypeStruct((t, dm), F32),
        compiler_params=_params("parallel"),
        name="glu_ln",
    )(z, wv, wg, h, g.reshape(1, dm), b.reshape(1, dm))


def _proj_ln_kernel(alpha, o_in_ref, w_ref, h_ref, g_ref, b_ref, o_ref):
    mix = jnp.dot(o_in_ref[...], w_ref[...], preferred_element_type=F32)
    o_ref[...] = _ln(alpha * h_ref[...] + mix, g_ref[...], b_ref[...])


def _proj_ln(alpha, o, w, h, g, b, tm=512):
    t, dm = h.shape
    row = _row_spec(tm, dm)
    vec = _full_spec((1, dm))
    return pl.pallas_call(
        functools.partial(_proj_ln_kernel, alpha),
        grid=(t // tm,),
        in_specs=[row, _full_spec((dm, dm)), row, vec, vec],
        out_specs=row,
        out_shape=jax.ShapeDtypeStruct((t, dm), F32),
        compiler_params=_params("parallel"),
        name="proj_ln",
    )(o, w, h, g.reshape(1, dm), b.reshape(1, dm))


def _ffn_ln_ple_kernel(alpha, h_ref, f_ref, g_ref, b_ref, wg_ref, p_ref, wp_ref, o_ref):
    h2 = _ln(alpha * h_ref[...] + f_ref[...], g_ref[...], b_ref[...])
    gate = jnp.dot(h2.astype(BF16), wg_ref[...], preferred_element_type=F32)
    emb = jnp.dot(p_ref[...].astype(BF16), wp_ref[...], preferred_element_type=F32)
    o_ref[...] = h2 + _sigmoid(gate) * emb


def _ffn_ln_ple(alpha, h, ffn, g, b, w_gate, p, w_ple, tm=512):
    t, dm = h.shape
    pd = p.shape[1]
    row = _row_spec(tm, dm)
    vec = _full_spec((1, dm))
    return pl.pallas_call(
        functools.partial(_ffn_ln_ple_kernel, alpha),
        grid=(t // tm,),
        in_specs=[row, row, vec, vec, _full_spec((dm, dm)), _row_spec(tm, pd), _full_spec((pd, dm))],
        out_specs=row,
        out_shape=jax.ShapeDtypeStruct((t, dm), F32),
        compiler_params=_params("parallel"),
        name="ffn_ln_ple",
    )(h, ffn, g.reshape(1, dm), b.reshape(1, dm), w_gate, p, w_ple)


def _s5_tables(lam_re, lam_im, log_dt, b_re, b_im, c_re, c_im):
    ell = SSM_CHUNK
    g, p = lam_re.shape
    dt = jnp.exp(log_dt)[:, None]

    def apow(k):
        mag = jnp.exp(k * lam_re * dt)
        ang = k * lam_im * dt
        return mag * jnp.cos(ang), mag * jnp.sin(ang)

    a_re, a_im = apow(1.0)
    den = lam_re * lam_re + lam_im * lam_im
    f_re = ((a_re - 1.0) * lam_re + a_im * lam_im) / den
    f_im = (a_im * lam_re - (a_re - 1.0) * lam_im) / den
    bb_re = f_re[..., None] * b_re - f_im[..., None] * b_im
    bb_im = f_re[..., None] * b_im + f_im[..., None] * b_re

    ks = jnp.arange(ell + 1, dtype=F32)[:, None, None]
    pw_re, pw_im = apow(ks)

    ca_re = c_re[None] * pw_re[:, :, None, :] - c_im[None] * pw_im[:, :, None, :]
    ca_im = c_re[None] * pw_im[:, :, None, :] + c_im[None] * pw_re[:, :, None, :]
    kk = (jnp.einsum('kgcp,gpd->kgcd', ca_re[:ell], bb_re)
          - jnp.einsum('kgcp,gpd->kgcd', ca_im[:ell], bb_im))

    rp_re, rp_im = pw_re[:ell][::-1], pw_im[:ell][::-1]
    ps_re = rp_re[..., None] * bb_re[None] - rp_im[..., None] * bb_im[None]
    ps_im = rp_re[..., None] * bb_im[None] + rp_im[..., None] * bb_re[None]
    pm = jnp.stack([ps_re, ps_im], axis=2)
    pm = pm.transpose(1, 0, 4, 2, 3).reshape(g, ell * SSM_GROUP, 2, p)

    q = jnp.stack([ca_re[1:], -ca_im[1:]], axis=3)
    q = q.transpose(1, 3, 4, 0, 2).reshape(g, 2, p, ell * SSM_GROUP)
    return kk, pm, q


def _s5_block_tables(lam_re, lam_im, log_dt, b_re, b_im, c_re, c_im, n_steps):
    ell, gc, p = SSM_CHUNK, SSM_GROUP, SSM_STATE
    g = lam_re.shape[0]
    gb = LANES // gc
    nb = g // gb
    kk, pm, q = _s5_tables(lam_re, lam_im, log_dt, b_re, b_im, c_re, c_im)
    same = jnp.eye(gb, dtype=bool)
    zero = jnp.zeros((), BF16)
    k6 = kk.astype(BF16).reshape(ell, nb, gb, gc, gc).transpose(1, 0, 2, 4, 3)
    k_blk = jnp.where(same[None, None, :, None, :, None], k6[:, :, :, :, None, :], zero)
    k_blk = k_blk.reshape(nb, ell, LANES, LANES)
    kz = jnp.concatenate([jnp.zeros((nb, 1, LANES, LANES), BF16), k_blk], axis=1)
    half = ell // 2
    top = jnp.concatenate([kz[:, 1::2], kz[:, 2::2]], axis=-1)
    bottom = jnp.concatenate([kz[:, 0::2][:, :half], kz[:, 1::2]], axis=-1)
    w_tab = jnp.concatenate([top, bottom], axis=-2)
    p6 = pm.astype(BF16).reshape(nb, gb, ell, gc, 2, p).transpose(0, 2, 1, 3, 4, 5)
    p_big = jnp.where(same[None, None, :, None, None, :, None], p6[:, :, :, :, :, None, :], zero)
    p_big = p_big.reshape(nb, ell * LANES, 2 * gb * p)
    q6 = q.astype(BF16).reshape(nb, gb, 2, p, ell, gc).transpose(0, 2, 1, 3, 4, 5)
    q_big = jnp.where(same[None, None, :, None, None, :, None], q6[:, :, :, :, :, None, :], zero)
    q_big = q_big.reshape(nb, 2 * gb * p, ell * LANES)
    dt = jnp.exp(log_dt)[:, None]
    ks = (ell * (2.0 ** jnp.arange(n_steps, dtype=F32)))[None, :, None]
    mag = jnp.exp(ks * (lam_re * dt)[:, None, :])
    ang = ks * (lam_im * dt)[:, None, :]
    are = (mag * jnp.cos(ang)).reshape(nb, gb, n_steps, p).transpose(0, 2, 1, 3).reshape(nb, n_steps, gb * p)
    aim = (mag * jnp.sin(ang)).reshape(nb, gb, n_steps, p).transpose(0, 2, 1, 3).reshape(nb, n_steps, gb * p)
    are2 = jnp.concatenate([are, are], axis=-1)
    aim2 = jnp.concatenate([-aim, aim], axis=-1)
    return w_tab, p_big, q_big, are2, aim2


def _s5_kernel(n_chunks, n_steps, u_ref, d_ref, w_ref, p_ref, q_ref, are_ref, aim_ref, z_ref):
    ell = SSM_CHUNK
    half = ell // 2
    xs = [jnp.concatenate([u_ref[j * n_chunks:(j + 1) * n_chunks, :].astype(BF16)
                           for j in (2 * jp, 2 * jp + 1)], axis=1) for jp in range(half)]
    xcat = jnp.concatenate(xs, axis=1)
    x = jnp.dot(xcat, p_ref[0], preferred_element_type=F32)
    width = x.shape[1]
    n_idx = lax.broadcasted_iota(jnp.int32, x.shape, 0)
    for s in range(n_steps):
        sh = 1 << s
        prev = jnp.where(n_idx >= sh, pltpu.roll(x, sh, 0), 0.0)
        swapped = pltpu.roll(prev, width // 2, 1)
        x = x + are_ref[0, s:s + 1, :] * prev + aim_ref[0, s:s + 1, :] * swapped
    x_start = jnp.where(n_idx >= 1, pltpu.roll(x, 1, 0), 0.0)
    y_state = jnp.dot(x_start.astype(BF16), q_ref[0], preferred_element_type=F32)
    for tp in range(half):
        y = y_state[:, 2 * tp * LANES:2 * (tp + 1) * LANES]
        for jp in range(tp + 1):
            y = y + jnp.dot(xs[jp], w_ref[0, tp - jp], preferred_element_type=F32)
        for tl in range(2):
            rows = slice((2 * tp + tl) * n_chunks, (2 * tp + tl + 1) * n_chunks)
            zj = jax.nn.gelu(y[:, tl * LANES:(tl + 1) * LANES] + d_ref[...] * u_ref[rows, :])
            z_ref[rows, :] = zj.astype(z_ref.dtype)


def _s5_gelu(u, d, bsz, lam_re, lam_im, log_dt, b_re, b_im, c_re, c_im):
    t, dm = u.shape
    seq = t // bsz
    n_chunks = seq // SSM_CHUNK
    assert n_chunks & (n_chunks - 1) == 0, "chunks per sequence must be a power of two"
    n_steps = max(1, (n_chunks - 1).bit_length())
    w_tab, p_big, q_big, are2, aim2 = _s5_block_tables(
        lam_re, lam_im, log_dt, b_re, b_im, c_re, c_im, n_steps)
    nb = dm // LANES
    kw = SSM_CHUNK * LANES
    sw = p_big.shape[2]
    half = SSM_CHUNK // 2
    return pl.pallas_call(
        functools.partial(_s5_kernel, n_chunks, n_steps),
        grid=(nb, bsz),
        in_specs=[
            pl.BlockSpec((seq, LANES), lambda i, b: (b, i)),
            pl.BlockSpec((1, LANES), lambda i, b: (0, i)),
            pl.BlockSpec((1, half, 2 * LANES, 2 * LANES), lambda i, b: (i, 0, 0, 0)),
            pl.BlockSpec((1, kw, sw), lambda i, b: (i, 0, 0)),
            pl.BlockSpec((1, sw, kw), lambda i, b: (i, 0, 0)),
            pl.BlockSpec((1, n_steps, sw), lambda i, b: (i, 0, 0)),
            pl.BlockSpec((1, n_steps, sw), lambda i, b: (i, 0, 0)),
        ],
        out_specs=pl.BlockSpec((seq, LANES), lambda i, b: (b, i)),
        out_shape=jax.ShapeDtypeStruct((t, dm), BF16),
        compiler_params=_params("parallel", "arbitrary"),
        name="s5_gelu",
    )(u, d.reshape(1, dm), w_tab, p_big, q_big, are2, aim2)


def _to_chunk_order(a, bsz):
    t, f = a.shape
    n = t // (bsz * SSM_CHUNK)
    return a.reshape(bsz, n, SSM_CHUNK, f).transpose(0, 2, 1, 3).reshape(t, f)


def _from_chunk_order(a, bsz):
    t, f = a.shape
    n = t // (bsz * SSM_CHUNK)
    return a.reshape(bsz, SSM_CHUNK, n, f).transpose(0, 2, 1, 3).reshape(t, f)


def _sb_pointwise(z2):
    pos = jnp.maximum(z2, 0.0)
    neg = z2 - pos
    soft = jnp.log2(1.0 + jnp.exp2(neg - pos))
    return pos + soft, neg - soft


def _sb_scores(qh, kb, tri, diag_mask):
    z2 = lax.dot_general(qh, kb, _NT, preferred_element_type=F32)
    neg_log_keep, log_beta = _sb_pointwise(z2)
    if diag_mask is not None:
        neg_log_keep = jnp.where(diag_mask, neg_log_keep, 0.0)
    later = jnp.dot(neg_log_keep.astype(BF16), tri, preferred_element_type=F32)
    arg = log_beta - later
    if diag_mask is not None:
        arg = jnp.where(diag_mask, arg, -jnp.inf)
    return arg, jnp.sum(neg_log_keep, axis=1, keepdims=True)


def _sb_kernel(bq, scale, q_ref, k_ref, v_ref, o_ref, arg_scr, acc_scr):
    qi = pl.program_id(2)
    n_heads = q_ref.shape[2] // HEAD_DIM
    row = lax.broadcasted_iota(jnp.int32, (bq, bq), 0)
    col = lax.broadcasted_iota(jnp.int32, (bq, bq), 1)
    tri = (row > col).astype(BF16)
    diag_mask = col < row
    lanes = [slice(hh * HEAD_DIM, (hh + 1) * HEAD_DIM) for hh in range(n_heads)]
    qs = [(q_ref[0, :, ls].astype(F32) * scale).astype(BF16) for ls in lanes]

    def rows_of(ref, kblk, ls):
        return ref[0, pl.ds(pl.multiple_of(kblk * bq, bq), bq), ls]

    def finish(hh, carry, vblk):
        w = jnp.exp2(arg_scr[hh] - carry)
        return jnp.dot(w.astype(BF16), rows_of(v_ref, vblk, lanes[hh]), preferred_element_type=F32)

    acc_scr[...] = jnp.zeros_like(acc_scr)
    state = []
    for hh in range(n_heads):
        arg, rowsum = _sb_scores(qs[hh], rows_of(k_ref, qi, lanes[hh]), tri, diag_mask)
        arg_scr[hh] = arg
        state.extend((jnp.zeros((bq, 1), F32), rowsum))

    def body(i, state):
        new = [None] * (2 * n_heads)
        group = 2
        for g0 in range(0, n_heads, group):
            heads = range(g0, min(g0 + group, n_heads))
            z2 = {hh: lax.dot_general(qs[hh], rows_of(k_ref, qi - 1 - i, lanes[hh]), _NT,
                                      preferred_element_type=F32) for hh in heads}
            w = {hh: jnp.exp2(arg_scr[hh] - state[2 * hh]).astype(BF16) for hh in heads}
            point = {hh: _sb_pointwise(z2[hh]) for hh in heads}
            for hh in heads:
                acc_scr[:, lanes[hh]] += jnp.dot(w[hh], rows_of(v_ref, qi - i, lanes[hh]),
                                                 preferred_element_type=F32)
            later = {hh: jnp.dot(point[hh][0].astype(BF16), tri, preferred_element_type=F32)
                     for hh in heads}
            for hh in heads:
                arg_scr[hh] = point[hh][1] - later[hh]
                new[2 * hh] = state[2 * hh] + state[2 * hh + 1]
                new[2 * hh + 1] = jnp.sum(point[hh][0], axis=1, keepdims=True)
        return tuple(new)

    state = lax.fori_loop(0, qi, body, tuple(state))
    outs = [acc_scr[:, lanes[hh]] + finish(hh, state[2 * hh], 0) for hh in range(n_heads)]
    o_ref[0] = jnp.concatenate(outs, axis=1).astype(o_ref.dtype)


def _stick_breaking(q, kv, bsz, bq=256, heads_per_step=8):
    t, dm = q.shape
    seq = t // bsz
    hw = heads_per_step * HEAD_DIM
    n_hb = dm // hw
    q3 = q.reshape(bsz, seq, dm)
    kv3 = kv.reshape(bsz, seq, 2 * dm)
    out = pl.pallas_call(
        functools.partial(_sb_kernel, bq, HEAD_DIM ** -0.5 * math.log2(math.e)),
        grid=(bsz, n_hb, seq // bq),
        in_specs=[
            pl.BlockSpec((1, bq, hw), lambda b, h, i: (b, i, h)),
            pl.BlockSpec((1, seq, hw), lambda b, h, i: (b, 0, h)),
            pl.BlockSpec((1, seq, hw), lambda b, h, i: (b, 0, n_hb + h)),
        ],
        out_specs=pl.BlockSpec((1, bq, hw), lambda b, h, i: (b, i, h)),
        out_shape=jax.ShapeDtypeStruct((bsz, seq, dm), BF16),
        scratch_shapes=[pltpu.VMEM((heads_per_step, bq, bq), F32), pltpu.VMEM((bq, hw), F32)],
        compiler_params=_params("parallel", "parallel", "arbitrary"),
        name="stick_breaking",
    )(q3, kv3, kv3)
    return out.reshape(t, dm)


def _cmpx(a, i, j):
    hi = jnp.maximum(a[i], a[j])
    lo = jnp.minimum(a[i], a[j])
    a[i], a[j] = hi, lo


def _bitonic_merge_desc(a):
    n = len(a)
    j = n // 2
    while j >= 1:
        for i in range(n):
            l = i ^ j
            if l > i:
                _cmpx(a, i, l)
        j //= 2


def _sorted_top16(s):
    k = PEER_TOPK
    a = [s[SUBLANES * i:SUBLANES * (i + 1), :] for i in range(k)]
    size = 2
    while size <= k:
        j = size // 2
        while j >= 1:
            for i in range(k):
                l = i ^ j
                if l > i:
                    if (i & size) == 0:
                        _cmpx(a, i, l)
                    else:
                        _cmpx(a, l, i)
            j //= 2
        size *= 2
    for shift in (4, 2, 1):
        b = [pltpu.roll(x, shift, 0) for x in a]
        a = [jnp.maximum(a[i], b[k - 1 - i]) for i in range(k)]
        _bitonic_merge_desc(a)
    return a


def _peer_select(v1, v2):
    k = PEER_TOPK
    cells = [v1[a] + v2[b] for a in range(k) for b in range(k) if (a + 1) * (b + 1) <= k]
    top = cells[0]
    cur = list(cells)
    remaining = jnp.full(top.shape, float(k), F32)
    tau = top
    neg = jnp.float32(-jnp.inf)
    for _ in range(k):
        m = functools.reduce(jnp.maximum, cur)
        eq = [c == m for c in cur]
        cnt = functools.reduce(jnp.add, [e.astype(F32) for e in eq])
        tau = jnp.where(remaining > 0.0, m, tau)
        remaining = remaining - cnt
        cur = [jnp.where(e, neg, c) for e, c in zip(eq, cur)]
    zsum = functools.reduce(
        jnp.add, [jnp.where(c >= tau, jnp.exp(c - top), 0.0) for c in cells])
    return tau, zsum


def _prefix_count(vals, test):
    m8 = test(vals[7])
    m4 = test(jnp.where(m8, vals[11], vals[3]))
    m2 = test(jnp.where(m8, jnp.where(m4, vals[13], vals[9]), jnp.where(m4, vals[5], vals[1])))
    hi = jnp.where(m4, jnp.where(m2, vals[14], vals[12]), jnp.where(m2, vals[10], vals[8]))
    lo = jnp.where(m4, jnp.where(m2, vals[6], vals[4]), jnp.where(m2, vals[2], vals[0]))
    m1 = test(jnp.where(m8, hi, lo))
    m16 = test(vals[15])
    count = jnp.where(m8, 8.0, 0.0) + jnp.where(m4, 4.0, 0.0) + jnp.where(m2, 2.0, 0.0)
    return count + jnp.where(m1, 1.0, 0.0) + jnp.where(m16, 1.0, 0.0)


def _split_bf16(x):
    hi = x.astype(BF16)
    lo = (x - hi.astype(F32)).astype(BF16)
    return hi, lo


def _scores_t(k_hi, k_lo, qh):
    q_hi, q_lo = _split_bf16(qh)
    s = lax.dot_general(k_hi, q_hi, _NT, preferred_element_type=F32)
    s = s + lax.dot_general(k_hi, q_lo, _NT, preferred_element_type=F32)
    return s + lax.dot_general(k_lo, q_hi, _NT, preferred_element_type=F32)


def _peer_kernel(n_e1, n_chunks, x_ref, wq_ref, k1_ref, k2_ref, u_ref, vt_ref, o_ref,
                 xt_scr, k1_scr, k2_scr, ga_a, ga_b, w_a, w_b, acc_scr):
    c = pl.program_id(1)
    tn = x_ref.shape[0]
    nk = PEER_NKEYS
    n_lt = tn // LANES

    @pl.when(c == 0)
    def _():
        xb = x_ref[...].astype(BF16)
        xt_scr[...] = x_ref[...].T.astype(BF16)
        q = jnp.dot(xb, wq_ref[...], preferred_element_type=F32)
        k1_hi, k1_lo = _split_bf16(k1_ref[...])
        k2_hi, k2_lo = _split_bf16(k2_ref[...])
        sub = lax.broadcasted_iota(jnp.int32, (SUBLANES, tn), 0)
        v1s = [jnp.zeros((SUBLANES, tn), F32) for _ in range(PEER_TOPK)]
        v2s = [jnp.zeros((SUBLANES, tn), F32) for _ in range(PEER_TOPK)]
        s1_all, s2_all = [], []
        for h in range(PEER_HEADS):
            s1 = _scores_t(k1_hi, k1_lo, q[:, 2 * nk * h:2 * nk * h + nk])
            s2 = _scores_t(k2_hi, k2_lo, q[:, 2 * nk * h + nk:2 * nk * (h + 1)])
            s1_all.append(s1)
            s2_all.append(s2)
            t1 = _sorted_top16(s1)
            t2 = _sorted_top16(s2)
            for a in range(PEER_TOPK):
                v1s[a] = jnp.where(sub == h, t1[a], v1s[a])
                v2s[a] = jnp.where(sub == h, t2[a], v2s[a])
        tau, zsum = _peer_select(v1s, v2s)
        inv_z = 1.0 / zsum
        for h in range(PEER_HEADS):
            s1, s2 = s1_all[h], s2_all[h]
            hr = slice(h, h + 1)
            v2_rows = [v2s[b][hr, :] for b in range(PEER_TOPK)]
            tau_h = tau[hr, :]
            c1 = _prefix_count(v2_rows, lambda row: s1 + row >= tau_h)
            r2 = _prefix_count(v2_rows, lambda row: row > s2)
            p1 = jnp.exp(s1 - v1s[0][hr, :]) * inv_z[hr, :]
            r2 = r2.astype(BF16)
            p2 = jnp.exp(s2 - v2s[0][hr, :]).astype(BF16)
            for lt in range(n_lt):
                ls = slice(lt * LANES, (lt + 1) * LANES)
                for kb in range(nk // SUBLANES):
                    rs = slice(kb * SUBLANES, (kb + 1) * SUBLANES)
                    k1_scr[kb, lt, h, 0] = c1[rs, ls]
                    k1_scr[kb, lt, h, 1] = p1[rs, ls]
                for jb in range(nk // PACKED_ROWS):
                    rs = slice(jb * PACKED_ROWS, (jb + 1) * PACKED_ROWS)
                    k2_scr[jb, lt, h, 0] = r2[rs, ls]
                    k2_scr[jb, lt, h, 1] = p2[rs, ls]
        acc_scr[...] = jnp.zeros_like(acc_scr)
        w_b[...] = jnp.zeros_like(w_b)
        act = jnp.dot(u_ref[...].astype(BF16), xt_scr[...], preferred_element_type=F32)
        ga_a[...] = jax.nn.gelu(act.astype(BF16))

    chunk1 = jnp.clip(c - 1, 0, n_chunks - 1)
    zero = jnp.zeros((PACKED_ROWS, LANES), BF16)

    def gate_tile(e, lt, ga_old, w_new):
        blk = chunk1 * (n_e1 // SUBLANES) + e // SUBLANES
        r = e % SUBLANES
        ls = slice(lt * LANES, (lt + 1) * LANES)
        c1b, p1b = [], []
        for h in range(PEER_HEADS):
            c1_row = k1_scr[blk, lt, h, 0][r:r + 1, :]
            p1_row = k1_scr[blk, lt, h, 1][r:r + 1, :]
            c1b.append(jnp.broadcast_to(c1_row, (PACKED_ROWS, LANES)).astype(BF16))
            p1b.append(jnp.broadcast_to(p1_row, (PACKED_ROWS, LANES)).astype(BF16))
        for jb in range(nk // PACKED_ROWS):
            g = zero
            for h in range(PEER_HEADS):
                g = g + p1b[h] * jnp.where(k2_scr[jb, lt, h, 0] < c1b[h], k2_scr[jb, lt, h, 1], zero)
            ws = slice(e * nk + jb * PACKED_ROWS, e * nk + (jb + 1) * PACKED_ROWS)
            w_new[ws, ls] = g * ga_old[ws, ls]

    def tick(ga_new, ga_old, w_new, w_old):
        e_sub = 2
        n_q = 8
        dm = vt_ref.shape[0]
        qrows = dm // n_q
        tiles = [(eo, lt) for eo in range(e_sub) for lt in range(n_lt)]
        per_q = len(tiles) // n_q
        parts = [None] * n_q
        for s in range(n_e1 // e_sub):
            rows = slice(s * e_sub * nk, (s + 1) * e_sub * nk)
            for q in range(n_q):
                d = jnp.dot(vt_ref[q * qrows:(q + 1) * qrows, rows], w_old[rows, :],
                            preferred_element_type=F32)
                parts[q] = d if parts[q] is None else parts[q] + d
                for eo, lt in tiles[q * per_q:(q + 1) * per_q]:
                    gate_tile(s * e_sub + eo, lt, ga_old, w_new)
            act = jnp.dot(u_ref[rows, :].astype(BF16), xt_scr[...], preferred_element_type=F32)
            ga_new[rows, :] = jax.nn.gelu(act.astype(BF16))
        for q in range(n_q):
            acc_scr[q * qrows:(q + 1) * qrows, :] += parts[q]

    full = jnp.logical_and(c >= 1, c <= n_chunks)

    @pl.when(jnp.logical_and(full, c % 2 == 0))
    def _():
        tick(ga_a, ga_b, w_b, w_a)

    @pl.when(jnp.logical_and(full, c % 2 == 1))
    def _():
        tick(ga_b, ga_a, w_a, w_b)

    @pl.when(c == n_chunks + 1)
    def _():
        w_last = w_b if (n_chunks - 1) % 2 == 1 else w_a
        out_t = acc_scr[...] + jnp.dot(vt_ref[...], w_last[...], preferred_element_type=F32)
        o_ref[...] = out_t.T


def _peer(x, wq, k1, k2, u_all, v_all, layer, tn=512, n_e1=8):
    assert n_e1 % SUBLANES == 0
    t, dm = x.shape
    n_exp = u_all.shape[1]
    ce = n_e1 * PEER_NKEYS
    n_chunks = n_exp // ce
    hq = wq.shape[1]
    vt_tab = v_all[layer].reshape(n_chunks, ce, dm).transpose(0, 2, 1).astype(BF16)
    n_lt = tn // LANES
    k1_tab = pltpu.VMEM((PEER_NKEYS // SUBLANES, n_lt, PEER_HEADS, 2, SUBLANES, LANES), F32)
    k2_tab = pltpu.VMEM((PEER_NKEYS // PACKED_ROWS, n_lt, PEER_HEADS, 2, PACKED_ROWS, LANES), BF16)
    last = n_chunks - 1
    return pl.pallas_call(
        functools.partial(_peer_kernel, n_e1, n_chunks),
        grid=(t // tn, n_chunks + 2),
        in_specs=[
            pl.BlockSpec((tn, dm), lambda i, c: (i, 0)),
            pl.BlockSpec((dm, hq), lambda i, c: (0, 0)),
            pl.BlockSpec((PEER_NKEYS, PEER_NKEYS), lambda i, c: (0, 0)),
            pl.BlockSpec((PEER_NKEYS, PEER_NKEYS), lambda i, c: (0, 0)),
            pl.BlockSpec((None, ce, dm), lambda i, c: (layer, jnp.minimum(c, last), 0)),
            pl.BlockSpec((None, dm, ce), lambda i, c: (jnp.clip(c - 2, 0, last), 0, 0)),
        ],
        out_specs=pl.BlockSpec((tn, dm), lambda i, c: (i, 0)),
        out_shape=jax.ShapeDtypeStruct((t, dm), F32),
        scratch_shapes=[
            pltpu.VMEM((dm, tn), BF16),
            k1_tab, k2_tab,
            pltpu.VMEM((ce, tn), BF16), pltpu.VMEM((ce, tn), BF16),
            pltpu.VMEM((ce, tn), BF16), pltpu.VMEM((ce, tn), BF16),
            pltpu.VMEM((dm, tn), F32),
        ],
        compiler_params=_params("parallel", "arbitrary"),
        name="peer",
    )(x, wq, k1, k2, u_all, vt_tab)


@jax.jit
def _trunk(x, p, ln_mix_g, ln_mix_b, ln_ffn_g, ln_ffn_b, ssm_w_in, ssm_lam_re, ssm_lam_im,
           ssm_log_dt, ssm_b_re, ssm_b_im, ssm_c_re, ssm_c_im, ssm_d, ssm_w_glu,
           kv_ln_g, kv_ln_b, w_kv, sb_w_q, sb_w_o, peer_w_q, peer_k1, peer_k2,
           peer_u, peer_v, ple_w, ple_gate):
    bsz, seq, dm = x.shape
    depth = p.shape[0]
    n_a = ssm_w_in.shape[0]
    alpha = (2 * depth) ** 0.25
    t = bsz * seq
    h = x.reshape(t, dm)
    if n_a > 0:
        h = _to_chunk_order(h, bsz)
    kv = None
    for i in range(depth):
        p_i = p[i].reshape(t, -1)
        if i < n_a:
            p_i = _to_chunk_order(p_i, bsz)
            u = _mm(h, ssm_w_in[i].astype(BF16))
            z = _s5_gelu(u, ssm_d[i], bsz, ssm_lam_re[i], ssm_lam_im[i], ssm_log_dt[i],
                         ssm_b_re[i], ssm_b_im[i], ssm_c_re[i], ssm_c_im[i])
            wglu = ssm_w_glu[i].astype(BF16)
            h = _glu_ln(alpha, z, wglu[:, :dm], wglu[:, dm:], h, ln_mix_g[i], ln_mix_b[i])
        else:
            if i == n_a:
                kv = _ln_mm(h, kv_ln_g, kv_ln_b, w_kv.astype(BF16), out_dtype=BF16)
            j = i - n_a
            q = _mm(h, sb_w_q[j].astype(BF16), out_dtype=BF16)
            o = _stick_breaking(q, kv, bsz)
            h = _proj_ln(alpha, o, sb_w_o[j].astype(BF16), h, ln_mix_g[i], ln_mix_b[i])
        ffn = _peer(h, peer_w_q[i].astype(BF16), peer_k1[i], peer_k2[i],
                    peer_u, peer_v, i)
        h = _ffn_ln_ple(alpha, h, ffn, ln_ffn_g[i], ln_ffn_b[i], ple_gate[i].astype(BF16),
                        p_i, ple_w[i].astype(BF16))
        if i == n_a - 1:
            h = _from_chunk_order(h, bsz)
    return h.reshape(bsz, seq, dm)


def kernel(x, p, ln_mix_g, ln_mix_b, ln_ffn_g, ln_ffn_b, ssm_w_in, ssm_lam_re, ssm_lam_im, ssm_log_dt, ssm_b_re, ssm_b_im, ssm_c_re, ssm_c_im, ssm_d, ssm_w_glu, kv_ln_g, kv_ln_b, w_kv, sb_w_q, sb_w_o, peer_w_q, peer_k1, peer_k2, peer_u, peer_v, ple_w, ple_gate):
    return _trunk(x, p, ln_mix_g, ln_mix_b, ln_ffn_g, ln_ffn_b, ssm_w_in, ssm_lam_re, ssm_lam_im,
                  ssm_log_dt, ssm_b_re, ssm_b_im, ssm_c_re, ssm_c_im, ssm_d, ssm_w_glu,
                  kv_ln_g, kv_ln_b, w_kv, sb_w_q, sb_w_o, peer_w_q, peer_k1, peer_k2,
                  peer_u, peer_v, ple_w, ple_gate)
```
